```python
import math
import jax, jax.numpy as jnp
from jax import lax
import numpy as np

D_MODEL = 2048
BATCH = 16
SEQ = 256
DEPTH = 1
DEC_BATCH = 8
DEC_SEQ = 1024
PAST_LEN = 512

GRID_W = 64
MIX_WIDTH = D_MODEL
ATTN_WIDTH = MIX_WIDTH // 2
DELTA_WIDTH = MIX_WIDTH - ATTN_WIDTH
H_A = 8
HD_A = ATTN_WIDTH // (2 * H_A)
DV_A = 2 * HD_A
H_B = 8
DK_B = DELTA_WIDTH // H_B
DV_B = DELTA_WIDTH // H_B
CONV_K = 5
CHUNK = 64
Q_BLOCK = 128
D_FF = 4 * D_MODEL
ROPE_BASE = 10000.0
EPS = 1e-6
N_MOD = 6
IN_SIZES = (2 * H_A * HD_A, 2 * H_A * HD_A, H_A * DV_A, 3 * DELTA_WIDTH, DELTA_WIDTH, 2 * H_B, 2 * H_B)
IN_COLS = 2 * H_A * HD_A * 2 + H_A * DV_A + 4 * DELTA_WIDTH + 4 * H_B

kernel_name = "hymba_diffattn_gdn_prefix_diffusion_step"


def rms_norm(x, g):
    x32 = x.astype(jnp.float32)
    y = x32 * lax.rsqrt(jnp.mean(x32 * x32, axis=-1, keepdims=True) + EPS) * g.astype(jnp.float32)
    return y.astype(x.dtype)


def l2_normalize(x):
    return x * lax.rsqrt(jnp.sum(x * x, axis=-1, keepdims=True) + EPS)


def axial_rope(x, rows):
    d = x.shape[-1]
    ax = d // 2
    nf = ax // 2
    inv = 1.0 / (ROPE_BASE ** (jnp.arange(nf, dtype=jnp.float32) * 2.0 / ax))
    row = jnp.repeat(jnp.arange(rows, dtype=jnp.float32), GRID_W)
    col = jnp.tile(jnp.arange(GRID_W, dtype=jnp.float32), rows)

    def rot(xa, pos):
        ang = pos[:, None] * inv[None, :]
        cos = jnp.cos(ang)[:, None, :]
        sin = jnp.sin(ang)[:, None, :]
        x1, x2 = xa[..., :nf], xa[..., nf:]
        return jnp.concatenate([x1 * cos - x2 * sin, x1 * sin + x2 * cos], axis=-1)

    x32 = x.astype(jnp.float32)
    out = jnp.concatenate([rot(x32[..., :ax], row), rot(x32[..., ax:], col)], axis=-1)
    return out.astype(x.dtype)


def diff_attention(q, k, v, lam):
    Bn, H, Sq, _, d = q.shape
    nb = Sq // Q_BLOCK
    qb = jnp.moveaxis(q.reshape(Bn, H, nb, Q_BLOCK, 2, d), 2, 0)
    scale = d ** -0.5

    def one_block(q_blk):
        s = jnp.einsum('bhqmd,bhkmd->bhmqk', q_blk, k).astype(jnp.float32) * scale
        p = jax.nn.softmax(s, axis=-1)
        pd = p[:, :, 0] - lam * p[:, :, 1]
        return jnp.einsum('bhqk,bhkv->bhqv', pd.astype(v.dtype), v)

    o = lax.map(one_block, qb)
    return jnp.moveaxis(o, 0, 2).reshape(Bn, H, Sq, v.shape[-1])


def short_conv(x, w):
    return lax.conv_general_dilated(
        x, w[:, None, :].astype(x.dtype), (1,), [(CONV_K // 2, CONV_K // 2)],
        dimension_numbers=('NWC', 'WIO', 'NWC'), feature_group_count=x.shape[-1])


def gdn_chunked(q, k, v, beta, g, s0):
    Bn, L, H, DK = q.shape
    DV = v.shape[-1]
    n = L // CHUNK

    def blocks(t):
        return jnp.moveaxis(t.reshape((Bn, n, CHUNK, H) + t.shape[3:]), 3, 1)

    q, k, v, beta, g = blocks(q), blocks(k), blocks(v), blocks(beta), blocks(g)
    G = jnp.cumsum(g, axis=-1)
    incl = jnp.tril(jnp.ones((CHUNK, CHUNK), dtype=bool))
    strict = jnp.tril(jnp.ones((CHUNK, CHUNK), dtype=bool), -1)
    diff = G[..., :, None] - G[..., None, :]
    decay = jnp.where(incl, jnp.exp(jnp.where(incl, diff, 0.0)), 0.0)
    kb = k * beta[..., None]
    vb = v * beta[..., None]
    lmat = jnp.where(strict, jnp.einsum('bhnid,bhnjd->bhnij', kb, k) * decay, 0.0)
    eye = jnp.eye(CHUNK, dtype=jnp.float32)
    rhs = jnp.concatenate([vb, kb * jnp.exp(G)[..., None]], axis=-1)
    sol = lax.linalg.triangular_solve(lmat + eye, rhs, left_side=True, lower=True, unit_diagonal=True)
    u, w = sol[..., :DV], sol[..., DV:]
    intra = jnp.einsum('bhnid,bhnjd->bhnij', q, k) * decay
    qg = q * jnp.exp(G)[..., None]
    g_last = G[..., -1]
    kg = k * jnp.exp(g_last[..., None] - G)[..., None]

    def step(S, xs):
        w_c, u_c, qg_c, kg_c, a_c, gl_c = xs
        v_new = u_c - jnp.einsum('bhck,bhkv->bhcv', w_c, S)
        o_c = jnp.einsum('bhck,bhkv->bhcv', qg_c, S) + jnp.einsum('bhij,bhjv->bhiv', a_c, v_new)
        S = jnp.exp(gl_c)[..., None, None] * S + jnp.einsum('bhck,bhcv->bhkv', kg_c, v_new)
        return S, o_c

    xs = tuple(jnp.moveaxis(t, 2, 0) for t in (w, u, qg, kg, intra, g_last))
    S, o = lax.scan(step, s0, xs)
    o = jnp.transpose(o, (1, 0, 3, 2, 4)).reshape(Bn, L, H, DV)
    return o, S


def delta_mixer(qkv_raw, gate, a_raw, b_raw, conv_w, a_log, dt_bias, norm_w, s0):
    Bn, L, _ = qkv_raw.shape
    qkv = jax.nn.silu(short_conv(qkv_raw, conv_w)).astype(jnp.float32)
    q, k, v = jnp.split(qkv, 3, axis=-1)
    q = l2_normalize(q.reshape(Bn, L, H_B, DK_B)) * (DK_B ** -0.5)
    k = l2_normalize(k.reshape(Bn, L, H_B, DK_B))
    v = v.reshape(Bn, L, H_B, DV_B)
    a = a_raw.astype(jnp.float32).reshape(Bn, L, 2, H_B)
    b = b_raw.astype(jnp.float32).reshape(Bn, L, 2, H_B)
    g = -jnp.exp(a_log.astype(jnp.float32)) * jax.nn.softplus(a + dt_bias.astype(jnp.float32))
    beta = jax.nn.sigmoid(b)
    s0 = s0.astype(jnp.float32)
    o_f, s_f = gdn_chunked(q, k, v, beta[:, :, 0], g[:, :, 0], s0[:, 0])
    fl = lambda t: jnp.flip(t, axis=1)
    o_b, s_b = gdn_chunked(fl(q), fl(k), fl(v), fl(beta[:, :, 1]), fl(g[:, :, 1]), s0[:, 1])
    o = o_f + fl(o_b)
    o = rms_norm(o, norm_w) * jax.nn.silu(gate.reshape(Bn, L, H_B, DV_B).astype(jnp.float32))
    return o.reshape(Bn, L, DELTA_WIDTH).astype(qkv_raw.dtype), jnp.stack([s_f, s_b], axis=1)


def trunk_layer(x, mod, lam_init, w_in, conv_w, a_log, dt_bias, delta_norm, attn_lambda,
                attn_subln, norm1, w_out, norm2, w_ff1, w_ff2, ctx=None):
    Bn, L, _ = x.shape
    sh1, sc1, g1, sh2, sc2, g2 = jnp.split(mod, N_MOD, axis=-1)
    xn = rms_norm(x, norm1) * (1 + sc1) + sh1
    proj = jnp.einsum('bld,dc->blc', xn, w_in)
    offs = []
    acc = 0
    for s in IN_SIZES[:-1]:
        acc += s
        offs.append(acc)
    qa, ka, va, qkv_b, gate_b, a_b, b_b = jnp.split(proj, offs, axis=-1)

    qa = qa.reshape(Bn, L, H_A, 2, HD_A).transpose(0, 2, 1, 3, 4)
    ka = ka.reshape(Bn, L, H_A, 2, HD_A).transpose(0, 2, 1, 3, 4)
    va = va.reshape(Bn, L, H_A, DV_A).transpose(0, 2, 1, 3)
    if ctx is None:
        keys, vals = ka, va
        s0 = jnp.zeros((Bn, 2, H_B, DK_B, DV_B), jnp.float32)
    else:
        ctx_k, ctx_v, s0 = ctx
        rows = L // GRID_W
        qa = axial_rope(qa, rows)
        keys = jnp.concatenate([ctx_k.astype(ka.dtype), axial_rope(ka, rows)], axis=2)
        vals = jnp.concatenate([ctx_v.astype(va.dtype), va], axis=2)
    lp = attn_lambda.astype(jnp.float32)
    lam = jnp.exp(jnp.sum(lp[0] * lp[1])) - jnp.exp(jnp.sum(lp[2] * lp[3])) + lam_init
    oa = diff_attention(qa, keys, vals, lam)
    oa = rms_norm(oa, attn_subln) * (1.0 - lam_init)
    oa = oa.transpose(0, 2, 1, 3).reshape(Bn, L, ATTN_WIDTH).astype(x.dtype)

    ob, s_final = delta_mixer(qkv_b, gate_b, a_b, b_b, conv_w, a_log, dt_bias, delta_norm, s0)

    mixed = jnp.einsum('blc,cd->bld', jnp.concatenate([oa, ob], axis=-1), w_out)
    h = x + g1 * mixed
    hn = rms_norm(h, norm2) * (1 + sc2) + sh2
    ff = jnp.square(jax.nn.relu(jnp.einsum('bld,df->blf', hn, w_ff1)))
    y = h + g2 * jnp.einsum('blf,fd->bld', ff, w_ff2)
    if ctx is None:
        return y, (ka, va, s_final)
    return y, None


def setup_inputs(seed: int = 0) -> dict:
    key = jax.random.key(seed)
    ks = jax.random.split(key, 24)
    nrm = jax.random.normal
    D = D_MODEL
    dt = jnp.exp(jax.random.uniform(ks[15], (DEPTH, 2, H_B), minval=math.log(1e-3), maxval=math.log(1e-1)))
    return {
        "x_prompt": nrm(ks[0], (BATCH, SEQ, D)),
        "x_sample": nrm(ks[1], (DEC_BATCH, DEC_SEQ, D)),
        "cache_attn_k": nrm(ks[2], (DEC_BATCH, DEPTH, H_A, PAST_LEN, 2, HD_A)),
        "cache_attn_v": nrm(ks[3], (DEC_BATCH, DEPTH, H_A, PAST_LEN, DV_A)),
        "state_delta": 0.5 * nrm(ks[4], (DEC_BATCH, DEPTH, 2, H_B, DK_B, DV_B)),
        "c": nrm(ks[5], (DEC_BATCH, D)),
        "c_ctx": nrm(ks[6], (D,)),
        "w_mod": 0.5 * D ** -0.5 * nrm(ks[7], (DEPTH, D, N_MOD * D)),
        "b_mod": 0.01 * nrm(ks[8], (DEPTH, N_MOD * D)),
        "norm1": 1.0 + 0.01 * nrm(ks[9], (DEPTH, D)),
        "w_in": D ** -0.5 * nrm(ks[10], (DEPTH, D, IN_COLS)),
        "conv_w": CONV_K ** -0.5 * nrm(ks[11], (DEPTH, CONV_K, 3 * DELTA_WIDTH)),
        "a_log": jnp.log(jax.random.uniform(ks[12], (DEPTH, 2, H_B), minval=1.0, maxval=16.0)),
        "dt_bias": dt + jnp.log(-jnp.expm1(-dt)),
        "delta_norm": 1.0 + 0.01 * nrm(ks[13], (DEPTH, DV_B)),
        "attn_lambda": 0.1 * nrm(ks[14], (DEPTH, 4, HD_A)),
        "attn_subln": 1.0 + 0.01 * nrm(ks[16], (DEPTH, DV_A)),
        "w_out": MIX_WIDTH ** -0.5 * nrm(ks[17], (DEPTH, MIX_WIDTH, D)),
        "norm2": 1.0 + 0.01 * nrm(ks[18], (DEPTH, D)),
        "w_ff1": D ** -0.5 * nrm(ks[19], (DEPTH, D, D_FF)),
        "w_ff2": D_FF ** -0.5 * nrm(ks[20], (DEPTH, D_FF, D)),
        "norm_f": 1.0 + 0.01 * nrm(ks[21], (D,)),
    }


def reference(x_prompt, x_sample, cache_attn_k, cache_attn_v, state_delta, c, c_ctx,
              w_mod, b_mod, norm1, w_in, conv_w, a_log, dt_bias, delta_norm, attn_lambda,
              attn_subln, w_out, norm2, w_ff1, w_ff2, norm_f):
    hp, hs = x_prompt, x_sample
    sc = jax.nn.silu(c)
    scc = jax.nn.silu(c_ctx)
    new_k, new_v, new_s = [], [], []
    for l in range(DEPTH):
        lam_init = 0.8 - 0.6 * math.exp(-0.3 * l)
        mod_ctx = (scc @ w_mod[l] + b_mod[l])[None, None, :]
        mod_lat = (sc @ w_mod[l] + b_mod[l])[:, None, :]
        params = (w_in[l], conv_w[l], a_log[l], dt_bias[l], delta_norm[l], attn_lambda[l],
                  attn_subln[l], norm1[l], w_out[l], norm2[l], w_ff1[l], w_ff2[l])
        hp, (k_l, v_l, s_l) = trunk_layer(hp, mod_ctx, lam_init, *params)
        hs, _ = trunk_layer(hs, mod_lat, lam_init, *params,
                            ctx=(cache_attn_k[:, l], cache_attn_v[:, l], state_delta[:, l]))
        new_k.append(k_l)
        new_v.append(v_l)
        new_s.append(s_l)
    y_prompt = rms_norm(hp, norm_f)
    y_sample = rms_norm(hs, norm_f)
    new_attn_k = jnp.stack(new_k, axis=1)
    new_attn_v = jnp.stack(new_v, axis=1)
    new_delta_state = jnp.stack(new_s, axis=1)
    return (y_prompt, y_sample, new_attn_k, new_attn_v, new_delta_state)
```

```python
import functools
import math

import jax
import jax.numpy as jnp
from jax import lax
from jax.experimental import pallas as pl
from jax.experimental.pallas import tpu as pltpu

F32 = jnp.float32
BF16 = jnp.bfloat16

D_MODEL = 2048
GRID_W = 64
H_A = 8
HD_A = 64
DV_A = 128
H_B = 8
DK_B = 128
DV_B = 128
CONV_K = 5
CHUNK = 64
D_FF = 4 * D_MODEL
ROPE_BASE = 10000.0
EPS = 1e-6
N_MOD = 6
LAM_INIT = 0.8 - 0.6 * math.exp(-0.3 * 0)

LANES = 128
N_MAIN = 7168
N_TAIL = 32
COL_Q, COL_K, COL_V = 0, 8, 16
COL_DQ, COL_DK, COL_DV, COL_GATE = 24, 32, 40, 48
VMEM_LIMIT = 56 * 1024 * 1024


def _cparams(sem):
    return pltpu.CompilerParams(dimension_semantics=sem, vmem_limit_bytes=VMEM_LIMIT)


def _sigmoid(x):
    return 1.0 / (1.0 + jnp.exp(-x))


def _silu(x):
    return x * _sigmoid(x)


def _softplus(x):
    return jnp.maximum(x, 0.0) + jnp.log(1.0 + jnp.exp(-jnp.abs(x)))


def _dot(a, b):
    return jnp.dot(a, b, preferred_element_type=F32)


def _dot_nt(a, b):
    return lax.dot_general(a, b, (((1,), (1,)), ((), ())), preferred_element_type=F32)


def _dot_tn(a, b):
    return lax.dot_general(a, b, (((0,), (0,)), ((), ())), preferred_element_type=F32)


def _dot_hi(a, b):
    return jnp.dot(a, b, preferred_element_type=F32, precision=lax.Precision.HIGHEST)


def _mod_kernel(c_ref, w_ref, b_ref, o_ref):
    s = _silu(c_ref[...]).astype(BF16)
    o_ref[...] = _dot(s, w_ref[...].astype(BF16)) + b_ref[...]


def _modulation(cc, w_mod, b_mod):
    rows = cc.shape[0]
    n = w_mod.shape[1]
    tn = 512
    return pl.pallas_call(
        _mod_kernel,
        grid=(n // tn,),
        in_specs=[pl.BlockSpec((rows, D_MODEL), lambda j: (0, 0)),
                  pl.BlockSpec((D_MODEL, tn), lambda j: (0, j)),
                  pl.BlockSpec((1, tn), lambda j: (0, j))],
        out_specs=pl.BlockSpec((rows, tn), lambda j: (0, j)),
        out_shape=jax.ShapeDtypeStruct((rows, n), F32),
        compiler_params=_cparams(("arbitrary",)),
        name="modulation",
    )(cc, w_mod, b_mod)


def _rope(x, cos, sin_lo, sin_hi):
    return (x * cos + pltpu.roll(x, LANES - 16, 1) * sin_lo + pltpu.roll(x, 16, 1) * sin_hi)


def _inproj_kernel(*refs, rope_tiles, tn):
    if rope_tiles:
        (x_ref, sh_ref, sc_ref, n1_ref, w_ref, wt_ref, cos_ref, slo_ref, shi_ref,
         o_ref, ot_ref, xn_ref) = refs
    else:
        x_ref, sh_ref, sc_ref, n1_ref, w_ref, wt_ref, o_ref, ot_ref, xn_ref = refs
    j = pl.program_id(1)

    @pl.when(j == 0)
    def _():
        x = x_ref[...]
        ms = jnp.mean(x * x, axis=-1, keepdims=True)
        y = x * lax.rsqrt(ms + EPS) * n1_ref[...]
        y = y * (1.0 + sc_ref[...]) + sh_ref[...]
        xn = y.astype(BF16)
        xn_ref[...] = xn
        ot_ref[...] = _dot(xn, wt_ref[...])

    r = _dot(xn_ref[...], w_ref[...])
    if rope_tiles:
        @pl.when(j < rope_tiles)
        def _():
            cos, slo, shi = cos_ref[...], slo_ref[...], shi_ref[...]
            for g in range(tn // LANES):
                sl = slice(g * LANES, (g + 1) * LANES)
                o_ref[:, sl] = _rope(r[:, sl], cos, slo, shi)

        @pl.when(j >= rope_tiles)
        def _():
            o_ref[...] = r
    else:
        o_ref[...] = r


def _in_projection(x2d, mod4, mod_row, norm1, w_main, w_tail, rope_tabs, seq_len, tm=512, tn=512):
    m = x2d.shape[0]
    tiles_per_seq = seq_len // tm
    rope_tiles = (2 * H_A * 2 * HD_A) // tn if rope_tabs is not None else 0
    in_specs = [
        pl.BlockSpec((tm, D_MODEL), lambda i, j: (i, 0)),
        pl.BlockSpec((None, None, 1, D_MODEL), lambda i, j: (mod_row(i), 0, 0, 0)),
        pl.BlockSpec((None, None, 1, D_MODEL), lambda i, j: (mod_row(i), 1, 0, 0)),
        pl.BlockSpec((1, D_MODEL), lambda i, j: (0, 0)),
        pl.BlockSpec((D_MODEL, tn), lambda i, j: (0, j)),
        pl.BlockSpec((D_MODEL, LANES), lambda i, j: (0, 0)),
    ]
    args = [x2d, mod4, mod4, norm1, w_main, w_tail]
    if rope_tabs is not None:
        for t in rope_tabs:
            in_specs.append(pl.BlockSpec((tm, LANES), lambda i, j: (i % tiles_per_seq, 0)))
            args.append(t)
    return pl.pallas_call(
        functools.partial(_inproj_kernel, rope_tiles=rope_tiles, tn=tn),
        grid=(m // tm, N_MAIN // tn),
        in_specs=in_specs,
        out_specs=[pl.BlockSpec((tm, tn), lambda i, j: (i, j)),
                   pl.BlockSpec((tm, LANES), lambda i, j: (i, 0))],
        out_shape=[jax.ShapeDtypeStruct((m, N_MAIN), F32),
                   jax.ShapeDtypeStruct((m, LANES), F32)],
        scratch_shapes=[pltpu.VMEM((tm, D_MODEL), BF16)],
        compiler_params=_cparams(("parallel", "arbitrary")),
        name="in_projection",
    )(*args)


def _rope_tables(seq_len):
    lane = jnp.arange(LANES)
    r = lane % HD_A
    axis = r // (HD_A // 2)
    u = r % (HD_A // 2)
    nf = HD_A // 4
    f = u % nf
    hi_half = u // nf
    inv = 1.0 / (ROPE_BASE ** (f.astype(F32) * 2.0 / (HD_A // 2)))
    t = jnp.arange(seq_len)
    pos = jnp.where(axis[None, :] == 0, (t // GRID_W)[:, None], (t % GRID_W)[:, None]).astype(F32)
    ang = pos * inv[None, :]
    cos, sin = jnp.cos(ang), jnp.sin(ang)
    sin_lo = jnp.where(hi_half[None, :] == 0, -sin, 0.0)
    sin_hi = jnp.where(hi_half[None, :] == 1, sin, 0.0)
    return cos, sin_lo, sin_hi


def _attn_kernel(*refs, has_ctx):
    if has_ctx:
        q_ref, k_ref, v_ref, ck_ref, cv_ref, lam_ref, g_ref, o_ref = refs
    else:
        q_ref, k_ref, v_ref, lam_ref, g_ref, o_ref = refs
    lp = lam_ref[...]
    lam = (jnp.exp(jnp.sum(lp[0:1] * lp[1:2], axis=-1, keepdims=True))
           - jnp.exp(jnp.sum(lp[2:3] * lp[3:4], axis=-1, keepdims=True)) + LAM_INIT)
    q = q_ref[...] * (HD_A ** -0.5)
    lane = lax.broadcasted_iota(jnp.int32, q.shape, 1)
    qs = (jnp.where(lane < HD_A, q, 0.0).astype(BF16), jnp.where(lane >= HD_A, q, 0.0).astype(BF16))
    keys = [k_ref[...].astype(BF16)]
    vals = [v_ref[...].astype(BF16)]
    if has_ctx:
        keys.insert(0, ck_ref[...].astype(BF16))
        vals.insert(0, cv_ref[...].astype(BF16))
    probs = []
    for qm in qs:
        s = [_dot_nt(qm, kk) for kk in keys]
        mx = functools.reduce(jnp.maximum, [jnp.max(x, axis=-1, keepdims=True) for x in s])
        e = [jnp.exp(x - mx) for x in s]
        z = functools.reduce(jnp.add, [jnp.sum(x, axis=-1, keepdims=True) for x in e])
        probs.append((e, 1.0 / z))
    (e0, r0), (e1, r1) = probs
    r1 = r1 * lam
    o = None
    for a, b, vv in zip(e0, e1, vals):
        pd = (a * r0 - b * r1).astype(BF16)
        t = _dot(pd, vv)
        o = t if o is None else o + t
    ms = jnp.mean(o * o, axis=-1, keepdims=True)
    o = o * lax.rsqrt(ms + EPS) * g_ref[...] * (1.0 - LAM_INIT)
    o_ref[...] = o.astype(o_ref.dtype)


def _diff_attention(proj3, ctx_k, ctx_v, attn_lambda, subln, tq):
    bsz, seq, _ = proj3.shape
    has_ctx = ctx_k is not None
    in_specs = [
        pl.BlockSpec((None, tq, LANES), lambda b, h, i: (b, i, COL_Q + h)),
        pl.BlockSpec((None, seq, LANES), lambda b, h, i: (b, 0, COL_K + h)),
        pl.BlockSpec((None, seq, LANES), lambda b, h, i: (b, 0, COL_V + h)),
    ]
    args = [proj3, proj3, proj3]
    if has_ctx:
        past = ctx_k.shape[2]
        in_specs += [pl.BlockSpec((None, None, past, LANES), lambda b, h, i: (b, h, 0, 0)),
                     pl.BlockSpec((None, None, past, LANES), lambda b, h, i: (b, h, 0, 0))]
        args += [ctx_k, ctx_v]
    in_specs += [pl.BlockSpec((4, HD_A), lambda b, h, i: (0, 0)),
                 pl.BlockSpec((1, DV_A), lambda b, h, i: (0, 0))]
    args += [attn_lambda, subln]
    return pl.pallas_call(
        functools.partial(_attn_kernel, has_ctx=has_ctx),
        grid=(bsz, H_A, seq // tq),
        in_specs=in_specs,
        out_specs=pl.BlockSpec((None, tq, LANES), lambda b, h, i: (b, i, h)),
        out_shape=jax.ShapeDtypeStruct((bsz, seq, H_A * DV_A), BF16),
        compiler_params=_cparams(("parallel", "parallel", "arbitrary")),
        name="diff_attention",
    )(*args)


def _shift_rows(x, s, row):
    n = x.shape[0]
    y = pltpu.roll(x, (-s) % n, 0)
    ok = (row + s >= 0) & (row + s < n)
    return jnp.where(ok, y, 0.0)


def _conv_silu(x, w, row):
    y = x * w[CONV_K // 2:CONV_K // 2 + 1]
    for t in range(CONV_K):
        s = t - CONV_K // 2
        if s != 0:
            y = y + _shift_rows(x, s, row) * w[t:t + 1]
    return _silu(y)


def _lane_col(x, col):
    lane = lax.broadcasted_iota(jnp.int32, x.shape, 1)
    c = jnp.sum(jnp.where(lane == col, x, 0.0), axis=-1, keepdims=True)
    return jnp.broadcast_to(c, x.shape)


def _delta_kernel(*refs, seq, has_s0, emit_state):
    it = iter(refs)
    qr_ref, kr_ref, vr_ref, gate_ref, ab_ref = (next(it) for _ in range(5))
    wq_ref, wk_ref, wv_ref, alog_ref, dtb_ref, nw_ref = (next(it) for _ in range(6))
    s0_ref = next(it) if has_s0 else None
    o_ref = next(it)
    st_ref = next(it) if emit_state else None
    (q_s, k_s, rhs_s, qg_s, kg_s, g_s, beta_s, egl_s, u_s, w_s, a_s, of_s, ob_s) = it

    h = pl.program_id(1)
    n_chunks = seq // CHUNK
    row = lax.broadcasted_iota(jnp.int32, (seq, LANES), 0)
    pos = row % CHUNK

    q = _conv_silu(qr_ref[...], wq_ref[...], row)
    k = _conv_silu(kr_ref[...], wk_ref[...], row)
    v = _conv_silu(vr_ref[...], wv_ref[...], row)
    q = q * lax.rsqrt(jnp.sum(q * q, axis=-1, keepdims=True) + EPS) * (DK_B ** -0.5)
    k = k * lax.rsqrt(jnp.sum(k * k, axis=-1, keepdims=True) + EPS)
    q_s[...] = q.astype(BF16)
    k_s[...] = k.astype(BF16)
    ab = ab_ref[...]
    g_all = -jnp.exp(alog_ref[...]) * _softplus(ab + dtb_ref[...])
    beta_all = _sigmoid(ab)
    for d in range(2):
        g = _lane_col(g_all, d * H_B + h)
        beta = _lane_col(beta_all, 2 * H_B + d * H_B + h)
        cs = g
        sft = 1
        while sft < CHUNK:
            if d == 0:
                cs = cs + jnp.where(pos >= sft, pltpu.roll(cs, sft, 0), 0.0)
            else:
                cs = cs + jnp.where(pos < CHUNK - sft, pltpu.roll(cs, seq - sft, 0), 0.0)
            sft *= 2
        tot = jnp.sum(g.reshape(n_chunks, CHUNK, LANES), axis=1, keepdims=True)
        tot = jnp.broadcast_to(tot, (n_chunks, CHUNK, LANES)).reshape(seq, LANES)
        eg = jnp.exp(cs)
        kb = k * beta
        rhs_s[d, :, 0:LANES] = (v * beta).astype(BF16)
        rhs_s[d, :, LANES:2 * LANES] = (kb * eg).astype(BF16)
        qg_s[d] = (q * eg).astype(BF16)
        kg_s[d] = (k * jnp.exp(tot - cs)).astype(BF16)
        g_s[d] = cs
        beta_s[d] = beta
        egl_s[d] = jnp.exp(tot)

    ri = lax.broadcasted_iota(jnp.int32, (CHUNK, CHUNK), 0)
    ci = lax.broadcasted_iota(jnp.int32, (CHUNK, CHUNK), 1)
    eye = (ri == ci).astype(F32)

    def chunk_terms(c, carry):
        r0 = pl.multiple_of(c * CHUNK, CHUNK)
        rows = pl.ds(r0, CHUNK)
        kc = k_s[rows, :]
        kk = _dot_nt(kc, kc)
        qk = _dot_nt(q_s[rows, :], kc)
        for d in range(2):
            incl = (ci <= ri) if d == 0 else (ci >= ri)
            strict = (ci < ri) if d == 0 else (ci > ri)
            gcol = g_s[d, rows, 0:CHUNK]
            grow = jnp.sum(eye * gcol, axis=0, keepdims=True)
            decay = jnp.where(incl, jnp.exp(jnp.where(incl, gcol - grow, 0.0)), 0.0)
            nm = jnp.where(strict, -(beta_s[d, rows, 0:CHUNK] * kk * decay), 0.0)
            t = eye + nm
            p = nm
            for _ in range(5):
                p = _dot_hi(p, p)
                t = t + _dot_hi(t, p)
            uw = _dot(t.astype(BF16), rhs_s[d, rows, :])
            u_s[d, rows, :] = uw[:, 0:LANES]
            w_s[d, rows, :] = uw[:, LANES:2 * LANES].astype(BF16)
            a_s[d, rows, :] = jnp.where(incl, qk * decay, 0.0).astype(BF16)
        return carry

    lax.fori_loop(0, n_chunks, chunk_terms, 0)

    def scan_step(t, carry):
        new = []
        for d in range(2):
            s = carry[d]
            c = t if d == 0 else n_chunks - 1 - t
            r0 = pl.multiple_of(c * CHUNK, CHUNK)
            rows = pl.ds(r0, CHUNK)
            sb = s.astype(BF16)
            ws = _dot(w_s[d, rows, :], sb)
            qs = _dot(qg_s[d, rows, :], sb)
            v_new = (u_s[d, rows, :] - ws).astype(BF16)
            o_c = qs + _dot(a_s[d, rows, :], v_new)
            if d == 0:
                of_s[rows, :] = o_c
            else:
                ob_s[rows, :] = o_c
            decay_last = egl_s[d, pl.ds(r0, 1), :]
            new.append(decay_last * s + _dot_tn(kg_s[d, rows, :], v_new))
        return tuple(new)

    if has_s0:
        init = (s0_ref[0], s0_ref[1])
    else:
        init = (jnp.zeros((DK_B, DV_B), F32), jnp.zeros((DK_B, DV_B), F32))
    s_f, s_b = lax.fori_loop(0, n_chunks, scan_step, init)
    if emit_state:
        st_ref[0] = s_f
        st_ref[1] = s_b

    o = of_s[...] + ob_s[...]
    ms = jnp.mean(o * o, axis=-1, keepdims=True)
    o = o * lax.rsqrt(ms + EPS) * nw_ref[...] * _silu(gate_ref[...])
    o_ref[...] = o.astype(o_ref.dtype)


def _delta_mixer(proj3, tail3, conv_w, alog_row, dtb_row, norm_w, s0, emit_state):
    bsz, seq, _ = proj3.shape
    has_s0 = s0 is not None

    def col(cb):
        return pl.BlockSpec((None, seq, LANES), lambda b, h: (b, 0, cb + h))

    def wcol(cb):
        return pl.BlockSpec((CONV_K, LANES), lambda b, h: (0, cb + h))

    in_specs = [col(COL_DQ), col(COL_DK), col(COL_DV), col(COL_GATE),
                pl.BlockSpec((None, seq, LANES), lambda b, h: (b, 0, 0)),
                wcol(0), wcol(H_B), wcol(2 * H_B),
                pl.BlockSpec((1, LANES), lambda b, h: (0, 0)),
                pl.BlockSpec((1, LANES), lambda b, h: (0, 0)),
                pl.BlockSpec((1, DV_B), lambda b, h: (0, 0))]
    args = [proj3, proj3, proj3, proj3, tail3, conv_w, conv_w, conv_w, alog_row, dtb_row, norm_w]
    state_spec = pl.BlockSpec((None, 2, None, DK_B, DV_B), lambda b, h: (b, 0, h, 0, 0))
    if has_s0:
        in_specs.append(state_spec)
        args.append(s0)
    out_specs = [pl.BlockSpec((None, seq, LANES), lambda b, h: (b, 0, h))]
    out_shape = [jax.ShapeDtypeStruct((bsz, seq, H_B * DV_B), BF16)]
    if emit_state:
        out_specs.append(state_spec)
        out_shape.append(jax.ShapeDtypeStruct((bsz, 2, H_B, DK_B, DV_B), F32))
    scratch = [
        pltpu.VMEM((seq, LANES), BF16),
        pltpu.VMEM((seq, LANES), BF16),
        pltpu.VMEM((2, seq, 2 * LANES), BF16),
        pltpu.VMEM((2, seq, LANES), BF16),
        pltpu.VMEM((2, seq, LANES), BF16),
        pltpu.VMEM((2, seq, LANES), F32),
        pltpu.VMEM((2, seq, LANES), F32),
        pltpu.VMEM((2, seq, LANES), F32),
        pltpu.VMEM((2, seq, LANES), F32),
        pltpu.VMEM((2, seq, LANES), BF16),
        pltpu.VMEM((2, seq, CHUNK), BF16),
        pltpu.VMEM((seq, LANES), F32),
        pltpu.VMEM((seq, LANES), F32),
    ]
    outs = pl.pallas_call(
        functools.partial(_delta_kernel, seq=seq, has_s0=has_s0, emit_state=emit_state),
        grid=(bsz, H_B),
        in_specs=in_specs,
        out_specs=out_specs,
        out_shape=out_shape,
        scratch_shapes=scratch,
        compiler_params=_cparams(("parallel", "parallel")),
        name="delta_mixer",
    )(*args)
    return outs if emit_state else (outs[0], None)


def _outproj_kernel(x_ref, oa_ref, ob_ref, g1_ref, wa_ref, wb_ref, h_ref):
    mixed = _dot(oa_ref[...], wa_ref[...]) + _dot(ob_ref[...], wb_ref[...])
    h_ref[...] = x_ref[...] + g1_ref[...] * mixed


def _out_projection(x2d, oa2d, ob2d, mod4, mod_row, w_out, tm=512, tn=1024):
    m = x2d.shape[0]
    half = w_out.shape[0] // 2
    return pl.pallas_call(
        _outproj_kernel,
        grid=(m // tm, D_MODEL // tn),
        in_specs=[pl.BlockSpec((tm, tn), lambda i, j: (i, j)),
                  pl.BlockSpec((tm, half), lambda i, j: (i, 0)),
                  pl.BlockSpec((tm, half), lambda i, j: (i, 0)),
                  pl.BlockSpec((None, None, 1, tn), lambda i, j: (mod_row(i), 2, 0, j)),
                  pl.BlockSpec((half, tn), lambda i, j: (0, j)),
                  pl.BlockSpec((half, tn), lambda i, j: (1, j))],
        out_specs=pl.BlockSpec((tm, tn), lambda i, j: (i, j)),
        out_shape=jax.ShapeDtypeStruct((m, D_MODEL), F32),
        compiler_params=_cparams(("parallel", "arbitrary")),
        name="out_projection",
    )(x2d, oa2d, ob2d, mod4, w_out, w_out)


def _mlp_kernel(h_ref, sh_ref, sc_ref, g2_ref, n2_ref, nf_ref, w1_ref, w2_ref, y_ref, hn_ref):
    f = pl.program_id(1)

    @pl.when(f == 0)
    def _():
        h = h_ref[...]
        ms = jnp.mean(h * h, axis=-1, keepdims=True)
        y = h * lax.rsqrt(ms + EPS) * n2_ref[...]
        hn_ref[...] = (y * (1.0 + sc_ref[...]) + sh_ref[...]).astype(BF16)

    a = jnp.maximum(_dot(hn_ref[...], w1_ref[...]), 0.0)
    part = _dot((a * a).astype(BF16), w2_ref[...])

    @pl.when(f == 0)
    def _():
        y_ref[...] = part

    @pl.when(f > 0)
    def _():
        y_ref[...] += part

    @pl.when(f == pl.num_programs(1) - 1)
    def _():
        y = h_ref[...] + g2_ref[...] * y_ref[...]
        ms = jnp.mean(y * y, axis=-1, keepdims=True)
        y_ref[...] = y * lax.rsqrt(ms + EPS) * nf_ref[...]


def _mlp(h2d, mod4, mod_row, norm2, norm_f, w1, w2, tm=512, tf=512):
    m = h2d.shape[0]

    def mod_spec(k):
        return pl.BlockSpec((None, None, 1, D_MODEL), lambda i, f: (mod_row(i), k, 0, 0))

    return pl.pallas_call(
        _mlp_kernel,
        grid=(m // tm, D_FF // tf),
        in_specs=[pl.BlockSpec((tm, D_MODEL), lambda i, f: (i, 0)),
                  mod_spec(3), mod_spec(4), mod_spec(5),
                  pl.BlockSpec((1, D_MODEL), lambda i, f: (0, 0)),
                  pl.BlockSpec((1, D_MODEL), lambda i, f: (0, 0)),
                  pl.BlockSpec((D_MODEL, tf), lambda i, f: (0, f)),
                  pl.BlockSpec((tf, D_MODEL), lambda i, f: (f, 0))],
        out_specs=pl.BlockSpec((tm, D_MODEL), lambda i, f: (i, 0)),
        out_shape=jax.ShapeDtypeStruct((m, D_MODEL), F32),
        scratch_shapes=[pltpu.VMEM((tm, D_MODEL), BF16)],
        compiler_params=_cparams(("parallel", "arbitrary")),
        name="mlp",
    )(h2d, mod4, mod4, mod4, norm2, norm_f, w1, w2)


def _trunk(x, mod4, mod_row, params, ctx):
    (w_main, w_tail, conv_w, alog_row, dtb_row, delta_norm, attn_lambda, attn_subln, norm1,
     w_out, norm2, w_ff1, w_ff2, norm_f) = params
    bsz, seq, _ = x.shape
    x2d = x.reshape(bsz * seq, D_MODEL)
    rope_tabs = _rope_tables(seq) if ctx is not None else None
    proj, tail = _in_projection(x2d, mod4, mod_row, norm1, w_main, w_tail, rope_tabs, seq)
    proj3 = proj.reshape(bsz, seq, N_MAIN)
    tail3 = tail.reshape(bsz, seq, LANES)
    if ctx is None:
        oa = _diff_attention(proj3, None, None, attn_lambda, attn_subln, tq=seq)
        ob, state = _delta_mixer(proj3, tail3, conv_w, alog_row, dtb_row, delta_norm, None, True)
    else:
        ctx_k, ctx_v, s0 = ctx
        oa = _diff_attention(proj3, ctx_k, ctx_v, attn_lambda, attn_subln, tq=256)
        ob, state = _delta_mixer(proj3, tail3, conv_w, alog_row, dtb_row, delta_norm, s0, False)
    h = _out_projection(x2d, oa.reshape(bsz * seq, -1), ob.reshape(bsz * seq, -1), mod4, mod_row, w_out)
    y = _mlp(h, mod4, mod_row, norm2, norm_f, w_ff1, w_ff2)
    return y.reshape(bsz, seq, D_MODEL), proj3, state


def kernel(x_prompt, x_sample, cache_attn_k, cache_attn_v, state_delta, c, c_ctx, w_mod, b_mod, norm1, w_in, conv_w, a_log, dt_bias, delta_norm, attn_lambda, attn_subln, w_out, norm2, w_ff1, w_ff2, norm_f):
    bp, lp, _ = x_prompt.shape
    bs, ls, _ = x_sample.shape
    past = cache_attn_k.shape[3]

    rows = 16
    cc = jnp.concatenate([c_ctx[None, :], c, jnp.zeros((rows - 1 - bs, D_MODEL), F32)], axis=0)
    mod = _modulation(cc, w_mod[0], b_mod[0][None, :])
    mod4 = mod.reshape(rows, N_MOD, 1, D_MODEL)

    w_main = w_in[0][:, :N_MAIN].astype(BF16)
    w_tail = jnp.pad(w_in[0][:, N_MAIN:], ((0, 0), (0, LANES - N_TAIL))).astype(BF16)
    alog_row = jnp.pad(a_log[0].reshape(1, 2 * H_B), ((0, 0), (0, LANES - 2 * H_B)))
    dtb_row = jnp.pad(dt_bias[0].reshape(1, 2 * H_B), ((0, 0), (0, LANES - 2 * H_B)))
    params = (w_main, w_tail, conv_w[0], alog_row, dtb_row, delta_norm[0][None, :], attn_lambda[0],
              attn_subln[0][None, :], norm1[0][None, :], w_out[0].astype(BF16), norm2[0][None, :],
              w_ff1[0].astype(BF16), w_ff2[0].astype(BF16), norm_f[None, :])

    tm = 512
    y_prompt, proj_p, state = _trunk(x_prompt, mod4, lambda i: 0, params, None)
    ctx = (cache_attn_k[:, 0].reshape(bs, H_A, past, 2 * HD_A), cache_attn_v[:, 0], state_delta[:, 0])
    y_sample, _, _ = _trunk(x_sample, mod4, lambda i: 1 + i // (ls // tm), params, ctx)

    new_k = proj_p[:, :, COL_K * LANES:(COL_K + H_A) * LANES].reshape(bp, lp, H_A, 2, HD_A)
    new_k = new_k.transpose(0, 2, 1, 3, 4)[:, None]
    new_v = proj_p[:, :, COL_V * LANES:(COL_V + H_A) * LANES].reshape(bp, lp, H_A, DV_A)
    new_v = new_v.transpose(0, 2, 1, 3)[:, None]
    return (y_prompt, y_sample, new_k, new_v, state[:, None])
```

```python
import functools
import math

import jax
import jax.numpy as jnp
from jax import lax
from jax.experimental import pallas as pl
from jax.experimental.pallas import tpu as pltpu

F32 = jnp.float32
BF16 = jnp.bfloat16

D_MODEL = 2048
GRID_W = 64
H_A = 8
HD_A = 64
DV_A = 128
H_B = 8
DK_B = 128
DV_B = 128
CONV_K = 5
CHUNK = 64
D_FF = 4 * D_MODEL
ROPE_BASE = 10000.0
EPS = 1e-6
N_MOD = 6
LAM_INIT = 0.8 - 0.6 * math.exp(-0.3 * 0)

LANES = 128
N_MAIN = 7168
N_TAIL = 32
COL_Q, COL_K, COL_V = 0, 8, 16
COL_DQ, COL_DK, COL_DV, COL_GATE = 24, 32, 40, 48
VMEM_LIMIT = 56 * 1024 * 1024
TOKEN_TILE = 1024
MLP_OUT_CHUNK = 512


def _cparams(sem):
    return pltpu.CompilerParams(dimension_semantics=sem, vmem_limit_bytes=VMEM_LIMIT)


def _sigmoid(x):
    return 1.0 / (1.0 + jnp.exp(-x))


def _silu(x):
    return x * _sigmoid(x)


def _softplus(x):
    return jnp.maximum(x, 0.0) + jnp.log(1.0 + jnp.exp(-jnp.abs(x)))


def _dot(a, b):
    return jnp.dot(a, b, preferred_element_type=F32)


def _dot_nt(a, b):
    return lax.dot_general(a, b, (((1,), (1,)), ((), ())), preferred_element_type=F32)


def _bmm(a, b):
    return lax.dot_general(a, b, (((2,), (1,)), ((0,), (0,))), preferred_element_type=F32)


def _bmm_nt(a, b):
    return lax.dot_general(a, b, (((2,), (2,)), ((0,), (0,))), preferred_element_type=F32)


def _mod_kernel(c_ref, w_ref, b_ref, o_ref):
    s = _silu(c_ref[...]).astype(BF16)
    o_ref[...] = _dot(s, w_ref[...].astype(BF16)) + b_ref[...]


def _modulation(cc, w_mod, b_mod):
    rows = cc.shape[0]
    n = w_mod.shape[1]
    tn = 512
    return pl.pallas_call(
        _mod_kernel,
        grid=(n // tn,),
        in_specs=[pl.BlockSpec((rows, D_MODEL), lambda j: (0, 0)),
                  pl.BlockSpec((D_MODEL, tn), lambda j: (0, j)),
                  pl.BlockSpec((1, tn), lambda j: (0, j))],
        out_specs=pl.BlockSpec((rows, tn), lambda j: (0, j)),
        out_shape=jax.ShapeDtypeStruct((rows, n), F32),
        compiler_params=_cparams(("arbitrary",)),
        name="modulation",
    )(cc, w_mod, b_mod)


def _rope(x, cos, sin_lo, sin_hi):
    return (x * cos + pltpu.roll(x, LANES - 16, 1) * sin_lo + pltpu.roll(x, 16, 1) * sin_hi)


def _inproj_kernel(*refs, rope_tiles, tn):
    if rope_tiles:
        (x_ref, sh_ref, sc_ref, n1_ref, w_ref, wt_ref, cos_ref, slo_ref, shi_ref,
         o_ref, ot_ref, xn_ref) = refs
    else:
        x_ref, sh_ref, sc_ref, n1_ref, w_ref, wt_ref, o_ref, ot_ref, xn_ref = refs
    j = pl.program_id(1)

    @pl.when(j == 0)
    def _():
        x = x_ref[...]
        ms = jnp.mean(x * x, axis=-1, keepdims=True)
        y = x * lax.rsqrt(ms + EPS) * n1_ref[...]
        y = y * (1.0 + sc_ref[...]) + sh_ref[...]
        xn = y.astype(BF16)
        xn_ref[...] = xn
        ot_ref[...] = _dot(xn, wt_ref[...])

    r = _dot(xn_ref[...], w_ref[...])
    if rope_tiles:
        @pl.when(j < rope_tiles)
        def _():
            cos, slo, shi = cos_ref[...], slo_ref[...], shi_ref[...]
            for g in range(tn // LANES):
                sl = slice(g * LANES, (g + 1) * LANES)
                o_ref[:, sl] = _rope(r[:, sl], cos, slo, shi)

        @pl.when(j >= rope_tiles)
        def _():
            o_ref[...] = r
    else:
        o_ref[...] = r


def _in_projection(x2d, mod4, mod_row, norm1, w_main, w_tail, rope_tabs, seq_len, tn=512):
    m = x2d.shape[0]
    tm = TOKEN_TILE
    tiles_per_seq = seq_len // tm
    rope_tiles = (2 * H_A * 2 * HD_A) // tn if rope_tabs is not None else 0
    in_specs = [
        pl.BlockSpec((tm, D_MODEL), lambda i, j: (i, 0)),
        pl.BlockSpec((None, None, 1, D_MODEL), lambda i, j: (mod_row(i), 0, 0, 0)),
        pl.BlockSpec((None, None, 1, D_MODEL), lambda i, j: (mod_row(i), 1, 0, 0)),
        pl.BlockSpec((1, D_MODEL), lambda i, j: (0, 0)),
        pl.BlockSpec((D_MODEL, tn), lambda i, j: (0, j)),
        pl.BlockSpec((D_MODEL, LANES), lambda i, j: (0, 0)),
    ]
    args = [x2d, mod4, mod4, norm1, w_main, w_tail]
    if rope_tabs is not None:
        for t in rope_tabs:
            in_specs.append(pl.BlockSpec((tm, LANES), lambda i, j: (i % tiles_per_seq, 0)))
            args.append(t)
    return pl.pallas_call(
        functools.partial(_inproj_kernel, rope_tiles=rope_tiles, tn=tn),
        grid=(m // tm, N_MAIN // tn),
        in_specs=in_specs,
        out_specs=[pl.BlockSpec((tm, tn), lambda i, j: (i, j)),
                   pl.BlockSpec((tm, LANES), lambda i, j: (i, 0))],
        out_shape=[jax.ShapeDtypeStruct((m, N_MAIN), F32),
                   jax.ShapeDtypeStruct((m, LANES), F32)],
        scratch_shapes=[pltpu.VMEM((tm, D_MODEL), BF16)],
        compiler_params=_cparams(("parallel", "arbitrary")),
        name="in_projection",
    )(*args)


def _rope_tables(seq_len):
    lane = jnp.arange(LANES)
    r = lane % HD_A
    axis = r // (HD_A // 2)
    u = r % (HD_A // 2)
    nf = HD_A // 4
    f = u % nf
    hi_half = u // nf
    inv = 1.0 / (ROPE_BASE ** (f.astype(F32) * 2.0 / (HD_A // 2)))
    t = jnp.arange(seq_len)
    pos = jnp.where(axis[None, :] == 0, (t // GRID_W)[:, None], (t % GRID_W)[:, None]).astype(F32)
    ang = pos * inv[None, :]
    cos, sin = jnp.cos(ang), jnp.sin(ang)
    sin_lo = jnp.where(hi_half[None, :] == 0, -sin, 0.0)
    sin_hi = jnp.where(hi_half[None, :] == 1, sin, 0.0)
    return cos, sin_lo, sin_hi


def _attn_kernel(*refs, has_ctx):
    if has_ctx:
        q_ref, k_ref, v_ref, ck_ref, cv_ref, lam_ref, g_ref, o_ref = refs
    else:
        q_ref, k_ref, v_ref, lam_ref, g_ref, o_ref = refs
    lp = lam_ref[...]
    lam = (jnp.exp(jnp.sum(lp[0:1] * lp[1:2], axis=-1, keepdims=True))
           - jnp.exp(jnp.sum(lp[2:3] * lp[3:4], axis=-1, keepdims=True)) + LAM_INIT)
    q = q_ref[...] * (HD_A ** -0.5)
    lane = lax.broadcasted_iota(jnp.int32, q.shape, 1)
    qs = (jnp.where(lane < HD_A, q, 0.0).astype(BF16), jnp.where(lane >= HD_A, q, 0.0).astype(BF16))
    keys = [k_ref[...].astype(BF16)]
    vals = [v_ref[...].astype(BF16)]
    if has_ctx:
        keys.insert(0, ck_ref[...].astype(BF16))
        vals.insert(0, cv_ref[...].astype(BF16))
    probs = []
    for qm in qs:
        s = [_dot_nt(qm, kk) for kk in keys]
        mx = functools.reduce(jnp.maximum, [jnp.max(x, axis=-1, keepdims=True) for x in s])
        e = [jnp.exp(x - mx) for x in s]
        z = functools.reduce(jnp.add, [jnp.sum(x, axis=-1, keepdims=True) for x in e])
        probs.append((e, 1.0 / z))
    (e0, r0), (e1, r1) = probs
    r1 = r1 * lam
    o = None
    for a, b, vv in zip(e0, e1, vals):
        pd = (a * r0 - b * r1).astype(BF16)
        t = _dot(pd, vv)
        o = t if o is None else o + t
    ms = jnp.mean(o * o, axis=-1, keepdims=True)
    o = o * lax.rsqrt(ms + EPS) * g_ref[...] * (1.0 - LAM_INIT)
    o_ref[...] = o.astype(o_ref.dtype)


def _diff_attention(proj3, ctx_k, ctx_v, attn_lambda, subln, tq):
    bsz, seq, _ = proj3.shape
    has_ctx = ctx_k is not None
    in_specs = [
        pl.BlockSpec((None, tq, LANES), lambda b, h, i: (b, i, COL_Q + h)),
        pl.BlockSpec((None, seq, LANES), lambda b, h, i: (b, 0, COL_K + h)),
        pl.BlockSpec((None, seq, LANES), lambda b, h, i: (b, 0, COL_V + h)),
    ]
    args = [proj3, proj3, proj3]
    if has_ctx:
        past = ctx_k.shape[2]
        in_specs += [pl.BlockSpec((None, None, past, LANES), lambda b, h, i: (b, h, 0, 0)),
                     pl.BlockSpec((None, None, past, LANES), lambda b, h, i: (b, h, 0, 0))]
        args += [ctx_k, ctx_v]
    in_specs += [pl.BlockSpec((4, HD_A), lambda b, h, i: (0, 0)),
                 pl.BlockSpec((1, DV_A), lambda b, h, i: (0, 0))]
    args += [attn_lambda, subln]
    return pl.pallas_call(
        functools.partial(_attn_kernel, has_ctx=has_ctx),
        grid=(bsz, H_A, seq // tq),
        in_specs=in_specs,
        out_specs=pl.BlockSpec((None, tq, LANES), lambda b, h, i: (b, i, h)),
        out_shape=jax.ShapeDtypeStruct((bsz, seq, H_A * DV_A), BF16),
        compiler_params=_cparams(("parallel", "parallel", "arbitrary")),
        name="diff_attention",
    )(*args)


GATE_G, GATE_BETA, GATE_TOT = 0, 2 * H_B, 4 * H_B
NEG_BIG = -1e30


def _gates_kernel(ab_ref, alog_ref, dtb_ref, o_ref, *, seq):
    n_chunks = seq // CHUNK
    ab = ab_ref[...]
    lane = lax.broadcasted_iota(jnp.int32, ab.shape, 1)
    pos = lax.broadcasted_iota(jnp.int32, ab.shape, 0) % CHUNK
    g = -jnp.exp(alog_ref[...]) * _softplus(ab + dtb_ref[...])
    g = jnp.where(lane < 2 * H_B, g, 0.0)
    fwd = g
    bwd = g
    sft = 1
    while sft < CHUNK:
        fwd = fwd + jnp.where(pos >= sft, pltpu.roll(fwd, sft, 0), 0.0)
        bwd = bwd + jnp.where(pos < CHUNK - sft, pltpu.roll(bwd, seq - sft, 0), 0.0)
        sft *= 2
    cs = jnp.where(lane < H_B, fwd, bwd)
    tot = jnp.sum(g.reshape(n_chunks, CHUNK, LANES), axis=1, keepdims=True)
    tot = jnp.broadcast_to(tot, (n_chunks, CHUNK, LANES)).reshape(seq, LANES)
    tot = pltpu.roll(tot, GATE_TOT, 1)
    out = jnp.where(lane < GATE_BETA, cs, jnp.where(lane < GATE_TOT, _sigmoid(ab), tot))
    o_ref[...] = out


def _delta_gates(tail3, alog_row, dtb_row):
    bsz, seq, _ = tail3.shape
    return pl.pallas_call(
        functools.partial(_gates_kernel, seq=seq),
        grid=(bsz,),
        in_specs=[pl.BlockSpec((None, seq, LANES), lambda b: (b, 0, 0)),
                  pl.BlockSpec((1, LANES), lambda b: (0, 0)),
                  pl.BlockSpec((1, LANES), lambda b: (0, 0))],
        out_specs=pl.BlockSpec((None, seq, LANES), lambda b: (b, 0, 0)),
        out_shape=jax.ShapeDtypeStruct((bsz, seq, LANES), F32),
        compiler_params=_cparams(("parallel",)),
        name="delta_gates",
    )(tail3, alog_row, dtb_row)


PAD_ROWS = 8


def _conv_silu(x_ref, cols, w, pad_ref):
    seq = x_ref.shape[0]
    pad_ref[PAD_ROWS:PAD_ROWS + seq, :] = x_ref[:, cols]
    y = None
    for t in range(CONV_K):
        s = PAD_ROWS + t - CONV_K // 2
        term = pad_ref[s:s + seq, :] * w[t:t + 1]
        y = term if y is None else y + term
    return _silu(y)


def _lane_col(x, col):
    lane = lax.broadcasted_iota(jnp.int32, x.shape, 1)
    c = jnp.sum(jnp.where(lane == col, x, 0.0), axis=-1, keepdims=True)
    return jnp.broadcast_to(c, x.shape)


def _delta_kernel(*refs, seq, hps, group, has_s0, emit_state):
    it = iter(refs)
    qr_ref, kr_ref, vr_ref, gate_ref, tok_ref = (next(it) for _ in range(5))
    wq_ref, wk_ref, wv_ref, nw_ref = (next(it) for _ in range(4))
    s0_ref = next(it) if has_s0 else None
    o_ref = next(it)
    st_ref = next(it) if emit_state else None
    kq_s, rhs_s, wq_s, akg_s, g_s, beta_s, egl_s, u_s, oacc_s, s_s, pad_s = it

    hg = pl.program_id(1)
    n_chunks = seq // CHUNK
    n_inst = hps * n_chunks
    ri = lax.broadcasted_iota(jnp.int32, (CHUNK, CHUNK), 0)
    ci = lax.broadcasted_iota(jnp.int32, (CHUNK, CHUNK), 1)
    eye = (ri == ci).astype(F32)
    tok = tok_ref[...]
    pad_s[0:PAD_ROWS, :] = jnp.zeros((PAD_ROWS, LANES), F32)
    pad_s[PAD_ROWS + seq:2 * PAD_ROWS + seq, :] = jnp.zeros((PAD_ROWS, LANES), F32)

    for hh in range(hps):
        cols = slice(hh * LANES, (hh + 1) * LANES)
        head = hg * hps + hh
        inst = slice(hh * n_chunks, (hh + 1) * n_chunks)
        rows = slice(hh * seq, (hh + 1) * seq)
        q = _conv_silu(qr_ref, cols, wq_ref[:, cols], pad_s)
        k = _conv_silu(kr_ref, cols, wk_ref[:, cols], pad_s)
        v = _conv_silu(vr_ref, cols, wv_ref[:, cols], pad_s)
        q = q * lax.rsqrt(jnp.sum(q * q, axis=-1, keepdims=True) + EPS) * (DK_B ** -0.5)
        k = k * lax.rsqrt(jnp.sum(k * k, axis=-1, keepdims=True) + EPS)
        kq_s[inst, 0:CHUNK, :] = k.astype(BF16).reshape(n_chunks, CHUNK, LANES)
        kq_s[inst, CHUNK:2 * CHUNK, :] = q.astype(BF16).reshape(n_chunks, CHUNK, LANES)
        for d in range(2):
            cs = _lane_col(tok, GATE_G + d * H_B + head)
            beta = _lane_col(tok, GATE_BETA + d * H_B + head)
            tot = _lane_col(tok, GATE_TOT + d * H_B + head)
            eg = jnp.exp(cs)
            rhs_s[d, rows, 0:LANES] = (v * beta).astype(BF16)
            rhs_s[d, rows, LANES:2 * LANES] = (k * (beta * eg)).astype(BF16)
            wq_s[d, inst, CHUNK:2 * CHUNK, :] = (q * eg).astype(BF16).reshape(n_chunks, CHUNK, LANES)
            kg = (k * jnp.exp(tot - cs)).reshape(n_chunks, CHUNK, LANES)
            for c in range(n_chunks):
                akg_s[d, hh * n_chunks + c, CHUNK:CHUNK + DK_B, :] = kg[c].T.astype(BF16)
            g_s[d, rows, :] = cs
            beta_s[d, rows, :] = beta
            egl_s[d, inst] = jnp.exp(tot.reshape(n_chunks, CHUNK, LANES)[:, 0:8, :])

    masks = [(ci <= ri)[None], (ci >= ri)[None]]
    for i0 in range(0, n_inst, group):
        inst = slice(i0, i0 + group)
        rows = slice(i0 * CHUNK, (i0 + group) * CHUNK)
        kq = kq_s[inst]
        kkqk = _bmm_nt(kq, kq[:, 0:CHUNK, :])
        kk, qk = kkqk[:, 0:CHUNK, :], kkqk[:, CHUNK:2 * CHUNK, :]
        nms = []
        for d in range(2):
            gcol = g_s[d, rows, 0:CHUNK].reshape(group, CHUNK, CHUNK)
            bcol = beta_s[d, rows, 0:CHUNK].reshape(group, CHUNK, CHUNK)
            grow = jnp.sum(eye[None] * gcol, axis=1, keepdims=True)
            decay = jnp.exp(jnp.where(masks[d], gcol - grow, NEG_BIG))
            akg_s[d, inst, 0:CHUNK, :] = (qk * decay).astype(BF16)
            nms.append(jnp.where((ri == ci)[None], 0.0, -(bcol * kk * decay)))
        nm = jnp.concatenate(nms, axis=0)
        t = eye[None] + nm
        nb = nm.astype(BF16)
        p = _bmm(nb, nb)
        for _ in range(4):
            pb = p.astype(BF16)
            tp = _bmm(jnp.concatenate([t.astype(BF16), pb], axis=1), pb)
            t = t + tp[:, 0:CHUNK, :]
            p = tp[:, CHUNK:2 * CHUNK, :]
        t = t + _bmm(t.astype(BF16), p.astype(BF16))
        tb = t.astype(BF16)
        resid = (eye[None] - t) + _bmm(nb, tb)
        t = t + _bmm(tb, resid.astype(BF16))
        rhs = rhs_s[:, rows, :].reshape(2 * group, CHUNK, 2 * LANES)
        uw = _bmm(t.astype(BF16), rhs)
        u_s[:, rows, :] = uw[:, :, 0:LANES].reshape(2, group * CHUNK, LANES)
        wq_s[:, inst, 0:CHUNK, :] = uw[:, :, LANES:2 * LANES].astype(BF16).reshape(2, group, CHUNK, LANES)

    for hh in range(hps):
        for d in range(2):
            s_s[d, hh] = s0_ref[d, hh] if has_s0 else jnp.zeros((DK_B, DV_B), F32)

    def scan_step(t, carry):
        for hh in range(hps):
            for d in range(2):
                c = hh * n_chunks + (t if d == 0 else n_chunks - 1 - t)
                rows = pl.ds(pl.multiple_of(c * CHUNK, CHUNK), CHUNK)
                s = s_s[d, hh]
                r = _dot(wq_s[d, c], s.astype(BF16))
                v_new = (u_s[d, rows, :] - r[0:CHUNK]).astype(BF16)
                r2 = _dot(akg_s[d, c], v_new)
                oacc_s[d, rows, :] = r[CHUNK:2 * CHUNK] + r2[0:CHUNK]
                s_s[d, hh] = egl_s[d, c][0:1, :] * s + r2[CHUNK:CHUNK + DK_B]
        return carry

    lax.fori_loop(0, n_chunks, scan_step, 0)

    for hh in range(hps):
        cols = slice(hh * LANES, (hh + 1) * LANES)
        rows = slice(hh * seq, (hh + 1) * seq)
        if emit_state:
            st_ref[0, hh] = s_s[0, hh]
            st_ref[1, hh] = s_s[1, hh]
        o = oacc_s[0, rows, :] + oacc_s[1, rows, :]
        ms = jnp.mean(o * o, axis=-1, keepdims=True)
        o = o * lax.rsqrt(ms + EPS) * nw_ref[...] * _silu(gate_ref[:, cols])
        o_ref[:, cols] = o.astype(o_ref.dtype)


def _delta_mixer(proj3, tok3, conv_w, norm_w, s0, emit_state):
    bsz, seq, _ = proj3.shape
    hps = max(2, min(H_B, 2048 // seq))
    has_s0 = s0 is not None
    n_chunks = seq // CHUNK
    n_inst = hps * n_chunks
    group = min(n_inst, 16)
    wide = hps * LANES

    def col(cb):
        return pl.BlockSpec((None, seq, wide), lambda b, g: (b, 0, cb // hps + g))

    def wcol(cb):
        return pl.BlockSpec((CONV_K, wide), lambda b, g: (0, cb // hps + g))

    in_specs = [col(COL_DQ), col(COL_DK), col(COL_DV), col(COL_GATE),
                pl.BlockSpec((None, seq, LANES), lambda b, g: (b, 0, 0)),
                wcol(0), wcol(H_B), wcol(2 * H_B),
                pl.BlockSpec((1, DV_B), lambda b, g: (0, 0))]
    args = [proj3, proj3, proj3, proj3, tok3, conv_w, conv_w, conv_w, norm_w]
    state_spec = pl.BlockSpec((None, 2, hps, DK_B, DV_B), lambda b, g: (b, 0, g, 0, 0))
    if has_s0:
        in_specs.append(state_spec)
        args.append(s0)
    out_specs = [pl.BlockSpec((None, seq, wide), lambda b, g: (b, 0, g))]
    out_shape = [jax.ShapeDtypeStruct((bsz, seq, H_B * DV_B), BF16)]
    if emit_state:
        out_specs.append(state_spec)
        out_shape.append(jax.ShapeDtypeStruct((bsz, 2, H_B, DK_B, DV_B), F32))
    scratch = [
        pltpu.VMEM((n_inst, 2 * CHUNK, LANES), BF16),
        pltpu.VMEM((2, hps * seq, 2 * LANES), BF16),
        pltpu.VMEM((2, n_inst, 2 * CHUNK, LANES), BF16),
        pltpu.VMEM((2, n_inst, CHUNK + DK_B, CHUNK), BF16),
        pltpu.VMEM((2, hps * seq, LANES), F32),
        pltpu.VMEM((2, hps * seq, LANES), F32),
        pltpu.VMEM((2, n_inst, 8, LANES), F32),
        pltpu.VMEM((2, hps * seq, LANES), F32),
        pltpu.VMEM((2, hps * seq, LANES), F32),
        pltpu.VMEM((2, hps, DK_B, DV_B), F32),
        pltpu.VMEM((seq + 2 * PAD_ROWS, LANES), F32),
    ]
    outs = pl.pallas_call(
        functools.partial(_delta_kernel, seq=seq, hps=hps, group=group, has_s0=has_s0,
                          emit_state=emit_state),
        grid=(bsz, H_B // hps),
        in_specs=in_specs,
        out_specs=out_specs,
        out_shape=out_shape,
        scratch_shapes=scratch,
        compiler_params=_cparams(("parallel", "parallel")),
        name="delta_mixer",
    )(*args)
    return outs if emit_state else (outs[0], None)


def _outproj_kernel(x_ref, oa_ref, ob_ref, g1_ref, wa_ref, wb_ref, h_ref):
    mixed = _dot(oa_ref[...], wa_ref[...]) + _dot(ob_ref[...], wb_ref[...])
    h_ref[...] = x_ref[...] + g1_ref[...] * mixed


def _out_projection(x2d, oa2d, ob2d, mod4, mod_row, w_out, tn=1024):
    m = x2d.shape[0]
    tm = TOKEN_TILE
    half = w_out.shape[0] // 2
    return pl.pallas_call(
        _outproj_kernel,
        grid=(m // tm, D_MODEL // tn),
        in_specs=[pl.BlockSpec((tm, tn), lambda i, j: (i, j)),
                  pl.BlockSpec((tm, half), lambda i, j: (i, 0)),
                  pl.BlockSpec((tm, half), lambda i, j: (i, 0)),
                  pl.BlockSpec((None, None, 1, tn), lambda i, j: (mod_row(i), 2, 0, j)),
                  pl.BlockSpec((half, tn), lambda i, j: (0, j)),
                  pl.BlockSpec((half, tn), lambda i, j: (1, j))],
        out_specs=pl.BlockSpec((tm, tn), lambda i, j: (i, j)),
        out_shape=jax.ShapeDtypeStruct((m, D_MODEL), F32),
        compiler_params=_cparams(("parallel", "arbitrary")),
        name="out_projection",
    )(x2d, oa2d, ob2d, mod4, w_out, w_out)


def _mlp_kernel(h_ref, sh_ref, sc_ref, g2_ref, n2_ref, nf_ref, w1_ref, w2_ref, y_ref, hn_ref):
    f = pl.program_id(1)

    @pl.when(f == 0)
    def _():
        h = h_ref[...]
        ms = jnp.mean(h * h, axis=-1, keepdims=True)
        y = h * lax.rsqrt(ms + EPS) * n2_ref[...]
        hn_ref[...] = (y * (1.0 + sc_ref[...]) + sh_ref[...]).astype(BF16)
        y_ref[...] = jnp.zeros_like(y_ref)

    a = jnp.maximum(_dot(hn_ref[...], w1_ref[...]), 0.0)
    a2 = (a * a).astype(BF16)
    for n0 in range(0, D_MODEL, MLP_OUT_CHUNK):
        sl = slice(n0, n0 + MLP_OUT_CHUNK)
        y_ref[:, sl] += _dot(a2, w2_ref[:, sl])

    @pl.when(f == pl.num_programs(1) - 1)
    def _():
        y = h_ref[...] + g2_ref[...] * y_ref[...]
        ms = jnp.mean(y * y, axis=-1, keepdims=True)
        y_ref[...] = y * lax.rsqrt(ms + EPS) * nf_ref[...]


def _mlp(h2d, mod4, mod_row, norm2, norm_f, w1, w2, tf=512):
    m = h2d.shape[0]
    tm = TOKEN_TILE

    def mod_spec(k):
        return pl.BlockSpec((None, None, 1, D_MODEL), lambda i, f: (mod_row(i), k, 0, 0))

    return pl.pallas_call(
        _mlp_kernel,
        grid=(m // tm, D_FF // tf),
        in_specs=[pl.BlockSpec((tm, D_MODEL), lambda i, f: (i, 0), pipeline_mode=pl.Buffered(1)),
                  mod_spec(3), mod_spec(4), mod_spec(5),
                  pl.BlockSpec((1, D_MODEL), lambda i, f: (0, 0)),
                  pl.BlockSpec((1, D_MODEL), lambda i, f: (0, 0)),
                  pl.BlockSpec((D_MODEL, tf), lambda i, f: (0, f)),
                  pl.BlockSpec((tf, D_MODEL), lambda i, f: (f, 0))],
        out_specs=pl.BlockSpec((tm, D_MODEL), lambda i, f: (i, 0)),
        out_shape=jax.ShapeDtypeStruct((m, D_MODEL), F32),
        scratch_shapes=[pltpu.VMEM((tm, D_MODEL), BF16)],
        compiler_params=_cparams(("parallel", "arbitrary")),
        name="mlp",
    )(h2d, mod4, mod4, mod4, norm2, norm_f, w1, w2)


def _trunk(x, mod4, mod_row, params, ctx):
    (w_main, w_tail, conv_w, alog_row, dtb_row, delta_norm, attn_lambda, attn_subln, norm1,
     w_out, norm2, w_ff1, w_ff2, norm_f) = params
    bsz, seq, _ = x.shape
    x2d = x.reshape(bsz * seq, D_MODEL)
    rope_tabs = _rope_tables(seq) if ctx is not None else None
    proj, tail = _in_projection(x2d, mod4, mod_row, norm1, w_main, w_tail, rope_tabs, seq)
    proj3 = proj.reshape(bsz, seq, N_MAIN)
    tok3 = _delta_gates(tail.reshape(bsz, seq, LANES), alog_row, dtb_row)
    if ctx is None:
        oa = _diff_attention(proj3, None, None, attn_lambda, attn_subln, tq=seq)
        ob, state = _delta_mixer(proj3, tok3, conv_w, delta_norm, None, True)
    else:
        ctx_k, ctx_v, s0 = ctx
        oa = _diff_attention(proj3, ctx_k, ctx_v, attn_lambda, attn_subln, tq=256)
        ob, state = _delta_mixer(proj3, tok3, conv_w, delta_norm, s0, False)
    h = _out_projection(x2d, oa.reshape(bsz * seq, -1), ob.reshape(bsz * seq, -1), mod4, mod_row, w_out)
    y = _mlp(h, mod4, mod_row, norm2, norm_f, w_ff1, w_ff2)
    return y.reshape(bsz, seq, D_MODEL), proj3, state


def kernel(x_prompt, x_sample, cache_attn_k, cache_attn_v, state_delta, c, c_ctx, w_mod, b_mod, norm1, w_in, conv_w, a_log, dt_bias, delta_norm, attn_lambda, attn_subln, w_out, norm2, w_ff1, w_ff2, norm_f):
    bp, lp, _ = x_prompt.shape
    bs, ls, _ = x_sample.shape
    past = cache_attn_k.shape[3]

    rows = 16
    cc = jnp.concatenate([c_ctx[None, :], c, jnp.zeros((rows - 1 - bs, D_MODEL), F32)], axis=0)
    mod = _modulation(cc, w_mod[0], b_mod[0][None, :])
    mod4 = mod.reshape(rows, N_MOD, 1, D_MODEL)

    w_main = w_in[0][:, :N_MAIN].astype(BF16)
    w_tail = jnp.pad(w_in[0][:, N_MAIN:], ((0, 0), (0, LANES - N_TAIL))).astype(BF16)
    alog_row = jnp.pad(a_log[0].reshape(1, 2 * H_B), ((0, 0), (0, LANES - 2 * H_B)))
    dtb_row = jnp.pad(dt_bias[0].reshape(1, 2 * H_B), ((0, 0), (0, LANES - 2 * H_B)))
    params = (w_main, w_tail, conv_w[0], alog_row, dtb_row, delta_norm[0][None, :], attn_lambda[0],
              attn_subln[0][None, :], norm1[0][None, :], w_out[0].astype(BF16), norm2[0][None, :],
              w_ff1[0].astype(BF16), w_ff2[0].astype(BF16), norm_f[None, :])

    y_prompt, proj_p, state = _trunk(x_prompt, mod4, lambda i: 0, params, None)
    ctx = (cache_attn_k[:, 0].reshape(bs, H_A, past, 2 * HD_A), cache_attn_v[:, 0], state_delta[:, 0])
    y_sample, _, _ = _trunk(x_sample, mod4, lambda i: 1 + i // (ls // TOKEN_TILE), params, ctx)

    new_k = proj_p[:, :, COL_K * LANES:(COL_K + H_A) * LANES].reshape(bp, lp, H_A, 2, HD_A)
    new_k = new_k.transpose(0, 2, 1, 3, 4)[:, None]
    new_v = proj_p[:, :, COL_V * LANES:(COL_V + H_A) * LANES].reshape(bp, lp, H_A, DV_A)
    new_v = new_v.transpose(0, 2, 1, 3)[:, None]
    return (y_prompt, y_sample, new_k, new_v, state[:, None])
```

```python
import functools
import math

import jax
import jax.numpy as jnp
from jax import lax
from jax.experimental import pallas as pl
from jax.experimental.pallas import tpu as pltpu

F32 = jnp.float32
BF16 = jnp.bfloat16

D_MODEL = 2048
GRID_W = 64
H_A = 8
HD_A = 64
DV_A = 128
H_B = 8
DK_B = 128
DV_B = 128
CONV_K = 5
CHUNK = 64
D_FF = 4 * D_MODEL
ROPE_BASE = 10000.0
EPS = 1e-6
N_MOD = 6
LAM_INIT = 0.8 - 0.6 * math.exp(-0.3 * 0)

LANES = 128
N_MAIN = 7168
N_TAIL = 32
ATTN_COLS = 3 * H_A * DV_A
PROJ_TILE = 512
ATTN_TILES = ATTN_COLS // PROJ_TILE
Q_SCALE = HD_A ** -0.5 * math.log2(math.e)
COL_DQ, COL_DK, COL_DV, COL_GATE = 0, 8, 16, 24
VMEM_LIMIT = 56 * 1024 * 1024
TOKEN_TILE = 1024
MLP_OUT_CHUNK = 512
ATTN_Q_TILE = 256


def _cparams(sem):
    return pltpu.CompilerParams(dimension_semantics=sem, vmem_limit_bytes=VMEM_LIMIT)


def _sigmoid(x):
    return 0.5 * jnp.tanh(0.5 * x) + 0.5


def _silu(x):
    return x * _sigmoid(x)


def _softplus(x):
    return jnp.maximum(x, 0.0) + jnp.log(1.0 + jnp.exp(-jnp.abs(x)))


def _dot(a, b):
    return jnp.dot(a, b, preferred_element_type=F32)


def _bmm(a, b):
    return lax.dot_general(a, b, (((2,), (1,)), ((0,), (0,))), preferred_element_type=F32)


def _bmm_nt(a, b):
    return lax.dot_general(a, b, (((2,), (2,)), ((0,), (0,))), preferred_element_type=F32)


def _mod_kernel(c_ref, w_ref, b_ref, o_ref):
    s = _silu(c_ref[...]).astype(BF16)
    o_ref[...] = _dot(s, w_ref[...].astype(BF16)) + b_ref[...]


def _modulation(cc, w_mod, b_mod):
    rows = cc.shape[0]
    n = w_mod.shape[1]
    tn = 512
    return pl.pallas_call(
        _mod_kernel,
        grid=(n // tn,),
        in_specs=[pl.BlockSpec((rows, D_MODEL), lambda j: (0, 0)),
                  pl.BlockSpec((D_MODEL, tn), lambda j: (0, j)),
                  pl.BlockSpec((1, tn), lambda j: (0, j))],
        out_specs=pl.BlockSpec((rows, tn), lambda j: (0, j)),
        out_shape=jax.ShapeDtypeStruct((rows, n), F32),
        compiler_params=_cparams(("arbitrary",)),
        name="modulation",
    )(cc, w_mod, b_mod)


def _rope(x, cos, sin_lo, sin_hi):
    return (x * cos + pltpu.roll(x, LANES - 16, 1) * sin_lo + pltpu.roll(x, 16, 1) * sin_hi)


def _inproj_norm(x_ref, sh_ref, sc_ref, n1_ref, wt_ref, ot_ref, xn_ref):
    x = x_ref[...]
    ms = jnp.mean(x * x, axis=-1, keepdims=True)
    y = x * lax.rsqrt(ms + EPS) * n1_ref[...]
    y = y * (1.0 + sc_ref[...]) + sh_ref[...]
    xn = y.astype(BF16)
    xn_ref[...] = xn
    ot_ref[...] = _dot(xn, wt_ref[...])


def _inproj_latent_kernel(x_ref, sh_ref, sc_ref, n1_ref, w_ref, wt_ref, cos_ref, slo_ref, shi_ref,
                          q_ref, kt_ref, v_ref, main_ref, ot_ref, xn_ref):
    j = pl.program_id(1)
    q_tiles = ATTN_TILES // 3
    heads = [slice(g * LANES, (g + 1) * LANES) for g in range(PROJ_TILE // LANES)]

    @pl.when(j == 0)
    def _():
        _inproj_norm(x_ref, sh_ref, sc_ref, n1_ref, wt_ref, ot_ref, xn_ref)

    r = _dot(xn_ref[...], w_ref[...])

    @pl.when(j < q_tiles)
    def _():
        cos, slo, shi = cos_ref[...], slo_ref[...], shi_ref[...]
        for sl in heads:
            q_ref[:, sl] = (_rope(r[:, sl], cos, slo, shi) * Q_SCALE).astype(BF16)

    @pl.when((j >= q_tiles) & (j < 2 * q_tiles))
    def _():
        cos, slo, shi = cos_ref[...], slo_ref[...], shi_ref[...]
        for g, sl in enumerate(heads):
            kt_ref[g] = _rope(r[:, sl], cos, slo, shi).T.astype(BF16)

    @pl.when((j >= 2 * q_tiles) & (j < ATTN_TILES))
    def _():
        v_ref[...] = r.astype(BF16)

    @pl.when(j >= ATTN_TILES)
    def _():
        main_ref[...] = r


def _inproj_context_kernel(x_ref, sh_ref, sc_ref, n1_ref, w_ref, wt_ref,
                           q_ref, k_ref, v_ref, main_ref, ot_ref, xn_ref, *, seq):
    j = pl.program_id(1)
    q_tiles = ATTN_TILES // 3

    @pl.when(j == 0)
    def _():
        _inproj_norm(x_ref, sh_ref, sc_ref, n1_ref, wt_ref, ot_ref, xn_ref)

    r = _dot(xn_ref[...], w_ref[...])

    def store_heads(dst_ref):
        for g in range(PROJ_TILE // LANES):
            dst_ref[:, g] = r[:, g * LANES:(g + 1) * LANES].reshape(-1, seq, LANES)

    @pl.when(j < q_tiles)
    def _():
        q_ref[...] = (r * Q_SCALE).astype(BF16)

    @pl.when((j >= q_tiles) & (j < 2 * q_tiles))
    def _():
        store_heads(k_ref)

    @pl.when((j >= 2 * q_tiles) & (j < ATTN_TILES))
    def _():
        store_heads(v_ref)

    @pl.when(j >= ATTN_TILES)
    def _():
        main_ref[...] = r


def _in_projection(x2d, mod4, mod_row, norm1, w_main, w_tail, rope_tabs, seq):
    m = x2d.shape[0]
    tm, tn = TOKEN_TILE, PROJ_TILE
    n_delta = N_MAIN - ATTN_COLS
    in_specs = [
        pl.BlockSpec((tm, D_MODEL), lambda i, j: (i, 0)),
        pl.BlockSpec((None, None, 1, D_MODEL), lambda i, j: (mod_row(i), 0, 0, 0)),
        pl.BlockSpec((None, None, 1, D_MODEL), lambda i, j: (mod_row(i), 1, 0, 0)),
        pl.BlockSpec((1, D_MODEL), lambda i, j: (0, 0)),
        pl.BlockSpec((D_MODEL, tn), lambda i, j: (0, j)),
        pl.BlockSpec((D_MODEL, LANES), lambda i, j: (0, 0)),
    ]
    args = [x2d, mod4, mod4, norm1, w_main, w_tail]
    main_spec = pl.BlockSpec((tm, tn), lambda i, j: (i, jnp.maximum(j - ATTN_TILES, 0)))
    main_shape = jax.ShapeDtypeStruct((m, n_delta), F32)
    tail_spec = pl.BlockSpec((tm, LANES), lambda i, j: (i, 0))
    tail_shape = jax.ShapeDtypeStruct((m, LANES), F32)
    q_tiles = ATTN_TILES // 3
    heads_per_tile = tn // LANES

    def cols_spec(first_tile):
        return pl.BlockSpec((tm, tn), lambda i, j: (i, jnp.clip(j - first_tile, 0, q_tiles - 1)))

    cols_shape = jax.ShapeDtypeStruct((m, ATTN_COLS // 3), BF16)
    if rope_tabs is not None:
        assert tm == seq
        for t in rope_tabs:
            in_specs.append(pl.BlockSpec((tm, LANES), lambda i, j: (0, 0)))
            args.append(t)
        body = _inproj_latent_kernel
        kt_spec = pl.BlockSpec((None, heads_per_tile, LANES, seq),
                               lambda i, j: (i, jnp.clip(j - q_tiles, 0, q_tiles - 1), 0, 0))
        out_specs = [cols_spec(0), kt_spec, cols_spec(2 * q_tiles), main_spec, tail_spec]
        out_shape = [cols_shape, jax.ShapeDtypeStruct((m // seq, H_A, LANES, seq), BF16), cols_shape,
                     main_shape, tail_shape]
    else:
        bt = tm // seq

        def head_spec(first_tile):
            return pl.BlockSpec((bt, heads_per_tile, seq, LANES),
                                lambda i, j: (i, jnp.clip(j - first_tile, 0, q_tiles - 1), 0, 0))

        head_shape = jax.ShapeDtypeStruct((m // seq, H_A, seq, LANES), F32)
        body = functools.partial(_inproj_context_kernel, seq=seq)
        out_specs = [cols_spec(0), head_spec(q_tiles), head_spec(2 * q_tiles), main_spec, tail_spec]
        out_shape = [cols_shape, head_shape, head_shape, main_shape, tail_shape]
    return pl.pallas_call(
        body,
        grid=(m // tm, N_MAIN // tn),
        in_specs=in_specs,
        out_specs=out_specs,
        out_shape=out_shape,
        scratch_shapes=[pltpu.VMEM((tm, D_MODEL), BF16)],
        compiler_params=_cparams(("parallel", "arbitrary")),
        name="in_projection",
    )(*args)


def _rope_tables(seq_len):
    lane = jnp.arange(LANES)
    r = lane % HD_A
    axis = r // (HD_A // 2)
    u = r % (HD_A // 2)
    nf = HD_A // 4
    f = u % nf
    hi_half = u // nf
    inv = 1.0 / (ROPE_BASE ** (f.astype(F32) * 2.0 / (HD_A // 2)))
    t = jnp.arange(seq_len)
    pos = jnp.where(axis[None, :] == 0, (t // GRID_W)[:, None], (t % GRID_W)[:, None]).astype(F32)
    ang = pos * inv[None, :]
    cos, sin = jnp.cos(ang), jnp.sin(ang)
    sin_lo = jnp.where(hi_half[None, :] == 0, -sin, 0.0)
    sin_hi = jnp.where(hi_half[None, :] == 1, sin, 0.0)
    return cos, sin_lo, sin_hi


def _attn_kernel(*refs, has_ctx):
    if has_ctx:
        q_ref, k_ref, v_ref, ck_ref, cv_ref, lam_ref, g_ref, o_ref = refs
    else:
        q_ref, k_ref, v_ref, lam_ref, g_ref, o_ref = refs
    lp = lam_ref[...]
    lam = (jnp.exp(jnp.sum(lp[0:1] * lp[1:2], axis=-1, keepdims=True))
           - jnp.exp(jnp.sum(lp[2:3] * lp[3:4], axis=-1, keepdims=True)) + LAM_INIT)
    first = (lax.broadcasted_iota(jnp.int32, (1, LANES), 1) < HD_A).astype(F32)
    keep = (first.astype(BF16), (1.0 - first).astype(BF16))
    if has_ctx:
        keys_t = [ck_ref[...], k_ref[...]]
        vals = [cv_ref[...], v_ref[...]]
    else:
        keys_t = [k_ref[...].T.astype(BF16)]
        vals = [v_ref[...].astype(BF16)]
    q = q_ref[...]
    probs = []
    for km in keep:
        s = [_dot(q * km, kt) for kt in keys_t]
        mx = functools.reduce(jnp.maximum, [jnp.max(x, axis=-1, keepdims=True) for x in s])
        e = [jnp.exp2(x - mx) for x in s]
        z = functools.reduce(jnp.add, [jnp.sum(x, axis=-1, keepdims=True) for x in e])
        probs.append((e, 1.0 / z))
    (e0, r0), (e1, r1) = probs
    r1 = r1 * lam
    o = None
    for a, b, vv in zip(e0, e1, vals):
        pd = (a * r0 - b * r1).astype(BF16)
        t = _dot(pd, vv)
        o = t if o is None else o + t
    ms = jnp.mean(o * o, axis=-1, keepdims=True)
    o = o * lax.rsqrt(ms + EPS) * g_ref[...] * (1.0 - LAM_INIT)
    o_ref[...] = o.astype(o_ref.dtype)


def _diff_attention(q3, k_src, v_src, ctx_kt, ctx_v, attn_lambda, subln, tq):
    bsz, seq, _ = q3.shape
    has_ctx = ctx_kt is not None
    head_major = pl.BlockSpec((None, None, seq, LANES), lambda b, h, i: (b, h, 0, 0))
    if has_ctx:
        k_spec = pl.BlockSpec((None, None, LANES, seq), lambda b, h, i: (b, h, 0, 0))
        v_spec = pl.BlockSpec((None, seq, LANES), lambda b, h, i: (b, 0, h))
    else:
        k_spec = v_spec = head_major
    in_specs = [pl.BlockSpec((None, tq, LANES), lambda b, h, i: (b, i, h)), k_spec, v_spec]
    args = [q3, k_src, v_src]
    if has_ctx:
        past = ctx_v.shape[2]
        in_specs += [pl.BlockSpec((None, None, LANES, past), lambda b, h, i: (b, h, 0, 0)),
                     pl.BlockSpec((None, None, past, LANES), lambda b, h, i: (b, h, 0, 0))]
        args += [ctx_kt, ctx_v]
    in_specs += [pl.BlockSpec((4, HD_A), lambda b, h, i: (0, 0)),
                 pl.BlockSpec((1, DV_A), lambda b, h, i: (0, 0))]
    args += [attn_lambda, subln]
    return pl.pallas_call(
        functools.partial(_attn_kernel, has_ctx=has_ctx),
        grid=(bsz, H_A, seq // tq),
        in_specs=in_specs,
        out_specs=pl.BlockSpec((None, tq, LANES), lambda b, h, i: (b, i, h)),
        out_shape=jax.ShapeDtypeStruct((bsz, seq, H_A * DV_A), BF16),
        compiler_params=_cparams(("parallel", "parallel", "arbitrary")),
        name="diff_attention",
    )(*args)


GATE_G, GATE_BETA, GATE_TOT = 0, 2 * H_B, 4 * H_B
NEG_BIG = -1e30


def _gates_kernel(ab_ref, alog_ref, dtb_ref, o_ref, *, seq):
    n_chunks = seq // CHUNK
    ab = ab_ref[...]
    lane = lax.broadcasted_iota(jnp.int32, ab.shape, 1)
    pos = lax.broadcasted_iota(jnp.int32, ab.shape, 0) % CHUNK
    g = -jnp.exp(alog_ref[...]) * _softplus(ab + dtb_ref[...])
    g = jnp.where(lane < 2 * H_B, g, 0.0)
    fwd = g
    bwd = g
    sft = 1
    while sft < CHUNK:
        fwd = fwd + jnp.where(pos >= sft, pltpu.roll(fwd, sft, 0), 0.0)
        bwd = bwd + jnp.where(pos < CHUNK - sft, pltpu.roll(bwd, seq - sft, 0), 0.0)
        sft *= 2
    cs = jnp.where(lane < H_B, fwd, bwd)
    tot = jnp.sum(g.reshape(n_chunks, CHUNK, LANES), axis=1, keepdims=True)
    tot = jnp.broadcast_to(tot, (n_chunks, CHUNK, LANES)).reshape(seq, LANES)
    tot = pltpu.roll(tot, GATE_TOT, 1)
    out = jnp.where(lane < GATE_BETA, cs, jnp.where(lane < GATE_TOT, _sigmoid(ab), tot))
    o_ref[...] = out


def _delta_gates(tail3, alog_row, dtb_row):
    bsz, seq, _ = tail3.shape
    return pl.pallas_call(
        functools.partial(_gates_kernel, seq=seq),
        grid=(bsz,),
        in_specs=[pl.BlockSpec((None, seq, LANES), lambda b: (b, 0, 0)),
                  pl.BlockSpec((1, LANES), lambda b: (0, 0)),
                  pl.BlockSpec((1, LANES), lambda b: (0, 0))],
        out_specs=pl.BlockSpec((None, seq, LANES), lambda b: (b, 0, 0)),
        out_shape=jax.ShapeDtypeStruct((bsz, seq, LANES), F32),
        compiler_params=_cparams(("parallel",)),
        name="delta_gates",
    )(tail3, alog_row, dtb_row)


PAD_ROWS = 8


def _conv_silu(x_ref, cols, w, pad_ref):
    seq = x_ref.shape[0]
    pad_ref[PAD_ROWS:PAD_ROWS + seq, :] = x_ref[:, cols]
    y = None
    for t in range(CONV_K):
        s = PAD_ROWS + t - CONV_K // 2
        term = pad_ref[s:s + seq, :] * w[t:t + 1]
        y = term if y is None else y + term
    return _silu(y)


def _lane_col(x, col):
    lane = lax.broadcasted_iota(jnp.int32, x.shape, 1)
    c = jnp.sum(jnp.where(lane == col, x, 0.0), axis=-1, keepdims=True)
    return jnp.broadcast_to(c, x.shape)


def _delta_kernel(*refs, seq, hps, group, has_s0, emit_state):
    it = iter(refs)
    qr_ref, kr_ref, vr_ref, gate_ref, tok_ref = (next(it) for _ in range(5))
    wq_ref, wk_ref, wv_ref, nw_ref = (next(it) for _ in range(4))
    s0_ref = next(it) if has_s0 else None
    o_ref = next(it)
    st_ref = next(it) if emit_state else None
    kq_s, rhs_s, wq_s, akg_s, g_s, beta_s, egl_s, u_s, oacc_s, s_s, pad_s = it

    hg = pl.program_id(1)
    n_chunks = seq // CHUNK
    n_inst = hps * n_chunks
    left = lax.broadcasted_iota(jnp.int32, (seq, LANES), 1) < CHUNK
    tok = tok_ref[...]
    pad_s[0:PAD_ROWS, :] = jnp.zeros((PAD_ROWS, LANES), F32)
    pad_s[PAD_ROWS + seq:2 * PAD_ROWS + seq, :] = jnp.zeros((PAD_ROWS, LANES), F32)

    for hh in range(hps):
        cols = slice(hh * LANES, (hh + 1) * LANES)
        head = hg * hps + hh
        inst = slice(hh * n_chunks, (hh + 1) * n_chunks)
        rows = slice(hh * seq, (hh + 1) * seq)
        q = _conv_silu(qr_ref, cols, wq_ref[:, cols], pad_s)
        k = _conv_silu(kr_ref, cols, wk_ref[:, cols], pad_s)
        v = _conv_silu(vr_ref, cols, wv_ref[:, cols], pad_s)
        q = q * lax.rsqrt(jnp.sum(q * q, axis=-1, keepdims=True) + EPS) * (DK_B ** -0.5)
        k = k * lax.rsqrt(jnp.sum(k * k, axis=-1, keepdims=True) + EPS)
        kq_s[inst, 0:CHUNK, :] = k.astype(BF16).reshape(n_chunks, CHUNK, LANES)
        kq_s[inst, CHUNK:2 * CHUNK, :] = q.astype(BF16).reshape(n_chunks, CHUNK, LANES)
        packed = []
        for d in range(2):
            cs = _lane_col(tok, GATE_G + d * H_B + head)
            beta = _lane_col(tok, GATE_BETA + d * H_B + head)
            tot = _lane_col(tok, GATE_TOT + d * H_B + head)
            eg = jnp.exp(cs)
            drows = slice(d * CHUNK, (d + 1) * CHUNK)
            rhs_s[inst, drows, 0:LANES] = (v * beta).astype(BF16).reshape(n_chunks, CHUNK, LANES)
            rhs_s[inst, drows, LANES:2 * LANES] = (k * (beta * eg)).astype(BF16).reshape(n_chunks, CHUNK, LANES)
            wq_s[d, inst, CHUNK:2 * CHUNK, :] = (q * eg).astype(BF16).reshape(n_chunks, CHUNK, LANES)
            kg = (k * jnp.exp(tot - cs)).reshape(n_chunks, CHUNK, LANES)
            packed.append((cs, beta, kg))
            egl_s[d, inst] = jnp.exp(tot.reshape(n_chunks, CHUNK, LANES)[:, 0:8, :])
        g_s[rows, :] = jnp.where(left, packed[0][0], packed[1][0])
        beta_s[rows, :] = jnp.where(left, packed[0][1], packed[1][1])
        for c in range(n_chunks):
            kg_pair = jnp.concatenate([packed[0][2][c], packed[1][2][c]], axis=0)
            akg_s[hh * n_chunks + c, CHUNK:CHUNK + DK_B, :] = kg_pair.T.astype(BF16)

    ri = lax.broadcasted_iota(jnp.int32, (CHUNK, LANES), 0)
    cj = lax.broadcasted_iota(jnp.int32, (CHUNK, LANES), 1)
    left2 = cj < CHUNK
    cj = jnp.where(left2, cj, cj - CHUNK)
    diag2 = (ri == cj)[None]
    eye2 = (ri == cj).astype(F32)[None]
    incl2 = (jnp.where(left2, ri - cj, cj - ri) >= 0)[None]
    keep_l = left2.astype(F32).astype(BF16)[None]
    keep_r = (1.0 - left2.astype(F32)).astype(BF16)[None]

    def blockdiag(x):
        return jnp.concatenate([x * keep_l, x * keep_r], axis=1)

    for i0 in range(0, n_inst, group):
        inst = slice(i0, i0 + group)
        rows = slice(i0 * CHUNK, (i0 + group) * CHUNK)
        kq = kq_s[inst]
        kc = kq[:, 0:CHUNK, :]
        kkqk = _bmm_nt(kq, jnp.concatenate([kc, kc], axis=1))
        kk, qk = kkqk[:, 0:CHUNK, :], kkqk[:, CHUNK:2 * CHUNK, :]
        gcol = g_s[rows, :].reshape(group, CHUNK, LANES)
        bcol = beta_s[rows, :].reshape(group, CHUNK, LANES)
        grow = jnp.sum(eye2 * gcol, axis=1, keepdims=True)
        decay = jnp.exp(jnp.where(incl2, gcol - grow, NEG_BIG))
        akg_s[inst, 0:CHUNK, :] = (qk * decay).astype(BF16)
        nm = jnp.where(diag2, 0.0, -(bcol * kk * decay))
        t = eye2 + nm
        nb = nm.astype(BF16)
        p = _bmm(nb, blockdiag(nb))
        for _ in range(4):
            pb = p.astype(BF16)
            tp = _bmm(jnp.concatenate([t.astype(BF16), pb], axis=1), blockdiag(pb))
            t = t + tp[:, 0:CHUNK, :]
            p = tp[:, CHUNK:2 * CHUNK, :]
        t = t + _bmm(t.astype(BF16), blockdiag(p.astype(BF16)))
        tb = t.astype(BF16)
        resid = (eye2 - t) + _bmm(nb, blockdiag(tb))
        t = t + _bmm(tb, blockdiag(resid.astype(BF16)))
        uw = _bmm(blockdiag(t.astype(BF16)), rhs_s[inst])
        for d in range(2):
            drows = slice(d * CHUNK, (d + 1) * CHUNK)
            u_s[d, rows, :] = uw[:, drows, 0:LANES].reshape(group * CHUNK, LANES)
            wq_s[d, inst, 0:CHUNK, :] = uw[:, drows, LANES:2 * LANES].astype(BF16)

    for hh in range(hps):
        for d in range(2):
            s_s[d, hh] = s0_ref[d, hh] if has_s0 else jnp.zeros((DK_B, DV_B), F32)

    zero_half = jnp.zeros((CHUNK, DV_B), BF16)

    def scan_step(t, carry):
        for hh in range(hps):
            for d in range(2):
                c = hh * n_chunks + (t if d == 0 else n_chunks - 1 - t)
                rows = pl.ds(pl.multiple_of(c * CHUNK, CHUNK), CHUNK)
                s = s_s[d, hh]
                r = _dot(wq_s[d, c], s.astype(BF16))
                v_new = (u_s[d, rows, :] - r[0:CHUNK]).astype(BF16)
                v_sel = jnp.concatenate([v_new, zero_half] if d == 0 else [zero_half, v_new], axis=0)
                r2 = _dot(akg_s[c], v_sel)
                oacc_s[d, rows, :] = r[CHUNK:2 * CHUNK] + r2[0:CHUNK]
                s_s[d, hh] = egl_s[d, c][0:1, :] * s + r2[CHUNK:CHUNK + DK_B]
        return carry

    lax.fori_loop(0, n_chunks, scan_step, 0)

    for hh in range(hps):
        cols = slice(hh * LANES, (hh + 1) * LANES)
        rows = slice(hh * seq, (hh + 1) * seq)
        if emit_state:
            st_ref[0, hh] = s_s[0, hh]
            st_ref[1, hh] = s_s[1, hh]
        o = oacc_s[0, rows, :] + oacc_s[1, rows, :]
        ms = jnp.mean(o * o, axis=-1, keepdims=True)
        o = o * lax.rsqrt(ms + EPS) * nw_ref[...] * _silu(gate_ref[:, cols])
        o_ref[:, cols] = o.astype(o_ref.dtype)


def _delta_mixer(proj3, tok3, conv_w, norm_w, s0, emit_state):
    bsz, seq, _ = proj3.shape
    hps = max(2, min(H_B, 2048 // seq))
    has_s0 = s0 is not None
    n_chunks = seq // CHUNK
    n_inst = hps * n_chunks
    group = min(n_inst, 16)
    wide = hps * LANES

    def col(cb):
        return pl.BlockSpec((None, seq, wide), lambda b, g: (b, 0, cb // hps + g))

    def wcol(cb):
        return pl.BlockSpec((CONV_K, wide), lambda b, g: (0, cb // hps + g))

    in_specs = [col(COL_DQ), col(COL_DK), col(COL_DV), col(COL_GATE),
                pl.BlockSpec((None, seq, LANES), lambda b, g: (b, 0, 0)),
                wcol(0), wcol(H_B), wcol(2 * H_B),
                pl.BlockSpec((1, DV_B), lambda b, g: (0, 0))]
    args = [proj3, proj3, proj3, proj3, tok3, conv_w, conv_w, conv_w, norm_w]
    state_spec = pl.BlockSpec((None, 2, hps, DK_B, DV_B), lambda b, g: (b, 0, g, 0, 0))
    if has_s0:
        in_specs.append(state_spec)
        args.append(s0)
    out_specs = [pl.BlockSpec((None, seq, wide), lambda b, g: (b, 0, g))]
    out_shape = [jax.ShapeDtypeStruct((bsz, seq, H_B * DV_B), BF16)]
    if emit_state:
        out_specs.append(state_spec)
        out_shape.append(jax.ShapeDtypeStruct((bsz, 2, H_B, DK_B, DV_B), F32))
    scratch = [
        pltpu.VMEM((n_inst, 2 * CHUNK, LANES), BF16),
        pltpu.VMEM((n_inst, 2 * CHUNK, 2 * LANES), BF16),
        pltpu.VMEM((2, n_inst, 2 * CHUNK, LANES), BF16),
        pltpu.VMEM((n_inst, CHUNK + DK_B, LANES), BF16),
        pltpu.VMEM((hps * seq, LANES), F32),
        pltpu.VMEM((hps * seq, LANES), F32),
        pltpu.VMEM((2, n_inst, 8, LANES), F32),
        pltpu.VMEM((2, hps * seq, LANES), F32),
        pltpu.VMEM((2, hps * seq, LANES), F32),
        pltpu.VMEM((2, hps, DK_B, DV_B), F32),
        pltpu.VMEM((seq + 2 * PAD_ROWS, LANES), F32),
    ]
    outs = pl.pallas_call(
        functools.partial(_delta_kernel, seq=seq, hps=hps, group=group, has_s0=has_s0,
                          emit_state=emit_state),
        grid=(bsz, H_B // hps),
        in_specs=in_specs,
        out_specs=out_specs,
        out_shape=out_shape,
        scratch_shapes=scratch,
        compiler_params=_cparams(("parallel", "parallel")),
        name="delta_mixer",
    )(*args)
    return outs if emit_state else (outs[0], None)


def _outproj_kernel(x_ref, oa_ref, ob_ref, g1_ref, wa_ref, wb_ref, h_ref):
    mixed = _dot(oa_ref[...], wa_ref[...]) + _dot(ob_ref[...], wb_ref[...])
    h_ref[...] = x_ref[...] + g1_ref[...] * mixed


def _out_projection(x2d, oa2d, ob2d, mod4, mod_row, w_out, tn=1024):
    m = x2d.shape[0]
    tm = TOKEN_TILE
    half = w_out.shape[0] // 2
    return pl.pallas_call(
        _outproj_kernel,
        grid=(m // tm, D_MODEL // tn),
        in_specs=[pl.BlockSpec((tm, tn), lambda i, j: (i, j)),
                  pl.BlockSpec((tm, half), lambda i, j: (i, 0)),
                  pl.BlockSpec((tm, half), lambda i, j: (i, 0)),
                  pl.BlockSpec((None, None, 1, tn), lambda i, j: (mod_row(i), 2, 0, j)),
                  pl.BlockSpec((half, tn), lambda i, j: (0, j)),
                  pl.BlockSpec((half, tn), lambda i, j: (1, j))],
        out_specs=pl.BlockSpec((tm, tn), lambda i, j: (i, j)),
        out_shape=jax.ShapeDtypeStruct((m, D_MODEL), F32),
        compiler_params=_cparams(("parallel", "arbitrary")),
        name="out_projection",
    )(x2d, oa2d, ob2d, mod4, w_out, w_out)


def _mlp_kernel(h_ref, sh_ref, sc_ref, g2_ref, n2_ref, nf_ref, w1_ref, w2_ref, y_ref, hn_ref):
    f = pl.program_id(1)

    @pl.when(f == 0)
    def _():
        h = h_ref[...]
        ms = jnp.mean(h * h, axis=-1, keepdims=True)
        y = h * lax.rsqrt(ms + EPS) * n2_ref[...]
        hn_ref[...] = (y * (1.0 + sc_ref[...]) + sh_ref[...]).astype(BF16)
        y_ref[...] = jnp.zeros_like(y_ref)

    a = jnp.maximum(_dot(hn_ref[...], w1_ref[...]), 0.0)
    a2 = (a * a).astype(BF16)
    for n0 in range(0, D_MODEL, MLP_OUT_CHUNK):
        sl = slice(n0, n0 + MLP_OUT_CHUNK)
        y_ref[:, sl] += _dot(a2, w2_ref[:, sl])

    @pl.when(f == pl.num_programs(1) - 1)
    def _():
        y = h_ref[...] + g2_ref[...] * y_ref[...]
        ms = jnp.mean(y * y, axis=-1, keepdims=True)
        y_ref[...] = y * lax.rsqrt(ms + EPS) * nf_ref[...]


def _mlp(h2d, mod4, mod_row, norm2, norm_f, w1, w2, tf=512):
    m = h2d.shape[0]
    tm = TOKEN_TILE

    def mod_spec(k):
        return pl.BlockSpec((None, None, 1, D_MODEL), lambda i, f: (mod_row(i), k, 0, 0))

    return pl.pallas_call(
        _mlp_kernel,
        grid=(m // tm, D_FF // tf),
        in_specs=[pl.BlockSpec((tm, D_MODEL), lambda i, f: (i, 0), pipeline_mode=pl.Buffered(1)),
                  mod_spec(3), mod_spec(4), mod_spec(5),
                  pl.BlockSpec((1, D_MODEL), lambda i, f: (0, 0)),
                  pl.BlockSpec((1, D_MODEL), lambda i, f: (0, 0)),
                  pl.BlockSpec((D_MODEL, tf), lambda i, f: (0, f)),
                  pl.BlockSpec((tf, D_MODEL), lambda i, f: (f, 0))],
        out_specs=pl.BlockSpec((tm, D_MODEL), lambda i, f: (i, 0)),
        out_shape=jax.ShapeDtypeStruct((m, D_MODEL), F32),
        scratch_shapes=[pltpu.VMEM((tm, D_MODEL), BF16)],
        compiler_params=_cparams(("parallel", "arbitrary")),
        name="mlp",
    )(h2d, mod4, mod4, mod4, norm2, norm_f, w1, w2)


def _trunk(x, mod4, mod_row, params, ctx):
    (w_main, w_tail, conv_w, alog_row, dtb_row, delta_norm, attn_lambda, attn_subln, norm1,
     w_out, norm2, w_ff1, w_ff2, norm_f) = params
    bsz, seq, _ = x.shape
    x2d = x.reshape(bsz * seq, D_MODEL)
    if ctx is None:
        q, new_k, new_v, main, tail = _in_projection(x2d, mod4, mod_row, norm1, w_main, w_tail, None, seq)
        oa = _diff_attention(q.reshape(bsz, seq, -1), new_k, new_v, None, None, attn_lambda, attn_subln,
                             tq=seq)
    else:
        ctx_kt, ctx_v, s0 = ctx
        new_k = new_v = None
        q, k_t, v, main, tail = _in_projection(x2d, mod4, mod_row, norm1, w_main, w_tail,
                                               _rope_tables(seq), seq)
        oa = _diff_attention(q.reshape(bsz, seq, -1), k_t, v.reshape(bsz, seq, -1), ctx_kt, ctx_v,
                             attn_lambda, attn_subln, tq=ATTN_Q_TILE)
    main3 = main.reshape(bsz, seq, N_MAIN - ATTN_COLS)
    tok3 = _delta_gates(tail.reshape(bsz, seq, LANES), alog_row, dtb_row)
    ob, state = _delta_mixer(main3, tok3, conv_w, delta_norm, None if ctx is None else s0, ctx is None)
    h = _out_projection(x2d, oa.reshape(bsz * seq, -1), ob.reshape(bsz * seq, -1), mod4, mod_row, w_out)
    y = _mlp(h, mod4, mod_row, norm2, norm_f, w_ff1, w_ff2)
    return y.reshape(bsz, seq, D_MODEL), new_k, new_v, state


def kernel(x_prompt, x_sample, cache_attn_k, cache_attn_v, state_delta, c, c_ctx, w_mod, b_mod, norm1, w_in, conv_w, a_log, dt_bias, delta_norm, attn_lambda, attn_subln, w_out, norm2, w_ff1, w_ff2, norm_f):
    bp, lp, _ = x_prompt.shape
    bs, ls, _ = x_sample.shape
    past = cache_attn_k.shape[3]

    rows = 16
    cc = jnp.concatenate([c_ctx[None, :], c, jnp.zeros((rows - 1 - bs, D_MODEL), F32)], axis=0)
    mod = _modulation(cc, w_mod[0], b_mod[0][None, :])
    mod4 = mod.reshape(rows, N_MOD, 1, D_MODEL)

    w_main = w_in[0][:, :N_MAIN].astype(BF16)
    w_tail = jnp.pad(w_in[0][:, N_MAIN:], ((0, 0), (0, LANES - N_TAIL))).astype(BF16)
    alog_row = jnp.pad(a_log[0].reshape(1, 2 * H_B), ((0, 0), (0, LANES - 2 * H_B)))
    dtb_row = jnp.pad(dt_bias[0].reshape(1, 2 * H_B), ((0, 0), (0, LANES - 2 * H_B)))
    params = (w_main, w_tail, conv_w[0], alog_row, dtb_row, delta_norm[0][None, :], attn_lambda[0],
              attn_subln[0][None, :], norm1[0][None, :], w_out[0].astype(BF16), norm2[0][None, :],
              w_ff1[0].astype(BF16), w_ff2[0].astype(BF16), norm_f[None, :])

    y_prompt, new_k, new_v, state = _trunk(x_prompt, mod4, lambda i: 0, params, None)
    ctx_kt = cache_attn_k[:, 0].reshape(bs, H_A, past, 2 * HD_A).astype(BF16).transpose(0, 1, 3, 2)
    ctx = (ctx_kt, cache_attn_v[:, 0].astype(BF16), state_delta[:, 0])
    y_sample, _, _, _ = _trunk(x_sample, mod4, lambda i: 1 + i // (ls // TOKEN_TILE), params, ctx)

    new_k = new_k.reshape(bp, 1, H_A, lp, 2, HD_A)
    new_v = new_v.reshape(bp, 1, H_A, lp, DV_A)
    return (y_prompt, y_sample, new_k, new_v, state[:, None])
```

```python
import functools
import math

import jax
import jax.numpy as jnp
from jax import lax
from jax.experimental import pallas as pl
from jax.experimental.pallas import tpu as pltpu

F32 = jnp.float32
BF16 = jnp.bfloat16

D_MODEL = 2048
GRID_W = 64
H_A = 8
HD_A = 64
DV_A = 128
H_B = 8
DK_B = 128
DV_B = 128
CONV_K = 5
CHUNK = 64
D_FF = 4 * D_MODEL
ROPE_BASE = 10000.0
EPS = 1e-6
N_MOD = 6
LAM_INIT = 0.8 - 0.6 * math.exp(-0.3 * 0)

LANES = 128
N_MAIN = 7168
N_TAIL = 32
ATTN_COLS = 3 * H_A * DV_A
PROJ_TILE = 512
ATTN_TILES = ATTN_COLS // PROJ_TILE
Q_SCALE = HD_A ** -0.5 * math.log2(math.e)
COL_DQ, COL_DK, COL_DV, COL_GATE = 0, 8, 16, 24
VMEM_LIMIT = 56 * 1024 * 1024
TOKEN_TILE = 1024
MLP_OUT_CHUNK = 512
ATTN_Q_TILE = 256


def _cparams(sem):
    return pltpu.CompilerParams(dimension_semantics=sem, vmem_limit_bytes=VMEM_LIMIT)


def _sigmoid(x):
    return 0.5 * jnp.tanh(0.5 * x) + 0.5


def _silu(x):
    return x * _sigmoid(x)


def _softplus(x):
    return jnp.maximum(x, 0.0) + jnp.log(1.0 + jnp.exp(-jnp.abs(x)))


def _dot(a, b):
    return jnp.dot(a, b, preferred_element_type=F32)


def _bmm(a, b):
    return lax.dot_general(a, b, (((2,), (1,)), ((0,), (0,))), preferred_element_type=F32)


def _bmm_nt(a, b):
    return lax.dot_general(a, b, (((2,), (2,)), ((0,), (0,))), preferred_element_type=F32)


def _mod_kernel(c_ref, w_ref, b_ref, o_ref):
    s = _silu(c_ref[...]).astype(BF16)
    o_ref[...] = _dot(s, w_ref[...].astype(BF16)) + b_ref[...]


def _modulation(cc, w_mod, b_mod):
    rows = cc.shape[0]
    n = w_mod.shape[1]
    tn = 512
    return pl.pallas_call(
        _mod_kernel,
        grid=(n // tn,),
        in_specs=[pl.BlockSpec((rows, D_MODEL), lambda j: (0, 0)),
                  pl.BlockSpec((D_MODEL, tn), lambda j: (0, j)),
                  pl.BlockSpec((1, tn), lambda j: (0, j))],
        out_specs=pl.BlockSpec((rows, tn), lambda j: (0, j)),
        out_shape=jax.ShapeDtypeStruct((rows, n), F32),
        compiler_params=_cparams(("arbitrary",)),
        name="modulation",
    )(cc, w_mod, b_mod)


def _rope(x, cos, sin_lo, sin_hi):
    return (x * cos + pltpu.roll(x, LANES - 16, 1) * sin_lo + pltpu.roll(x, 16, 1) * sin_hi)


def _inproj_norm(x_ref, sh_ref, sc_ref, n1_ref, wt_ref, ot_ref, xn_ref):
    x = x_ref[...]
    ms = jnp.mean(x * x, axis=-1, keepdims=True)
    y = x * lax.rsqrt(ms + EPS) * n1_ref[...]
    y = y * (1.0 + sc_ref[...]) + sh_ref[...]
    xn = y.astype(BF16)
    xn_ref[...] = xn
    ot_ref[...] = _dot(xn, wt_ref[...])


def _inproj_latent_kernel(x_ref, sh_ref, sc_ref, n1_ref, w_ref, wt_ref, cos_ref, slo_ref, shi_ref,
                          q_ref, kt_ref, v_ref, main_ref, ot_ref, xn_ref):
    j = pl.program_id(1)
    q_tiles = ATTN_TILES // 3
    heads = [slice(g * LANES, (g + 1) * LANES) for g in range(PROJ_TILE // LANES)]

    @pl.when(j == 0)
    def _():
        _inproj_norm(x_ref, sh_ref, sc_ref, n1_ref, wt_ref, ot_ref, xn_ref)

    r = _dot(xn_ref[...], w_ref[...])

    @pl.when(j < q_tiles)
    def _():
        cos, slo, shi = cos_ref[...], slo_ref[...], shi_ref[...]
        for sl in heads:
            q_ref[:, sl] = (_rope(r[:, sl], cos, slo, shi) * Q_SCALE).astype(BF16)

    @pl.when((j >= q_tiles) & (j < 2 * q_tiles))
    def _():
        cos, slo, shi = cos_ref[...], slo_ref[...], shi_ref[...]
        for g, sl in enumerate(heads):
            kt_ref[g] = _rope(r[:, sl], cos, slo, shi).T.astype(BF16)

    @pl.when((j >= 2 * q_tiles) & (j < ATTN_TILES))
    def _():
        v_ref[...] = r.astype(BF16)

    @pl.when(j >= ATTN_TILES)
    def _():
        main_ref[...] = r


def _inproj_context_kernel(x_ref, sh_ref, sc_ref, n1_ref, w_ref, wt_ref,
                           q_ref, k_ref, v_ref, main_ref, ot_ref, xn_ref, *, seq):
    j = pl.program_id(1)
    q_tiles = ATTN_TILES // 3

    @pl.when(j == 0)
    def _():
        _inproj_norm(x_ref, sh_ref, sc_ref, n1_ref, wt_ref, ot_ref, xn_ref)

    r = _dot(xn_ref[...], w_ref[...])

    def store_heads(dst_ref):
        for g in range(PROJ_TILE // LANES):
            dst_ref[:, g] = r[:, g * LANES:(g + 1) * LANES].reshape(-1, seq, LANES)

    @pl.when(j < q_tiles)
    def _():
        q_ref[...] = (r * Q_SCALE).astype(BF16)

    @pl.when((j >= q_tiles) & (j < 2 * q_tiles))
    def _():
        store_heads(k_ref)

    @pl.when((j >= 2 * q_tiles) & (j < ATTN_TILES))
    def _():
        store_heads(v_ref)

    @pl.when(j >= ATTN_TILES)
    def _():
        main_ref[...] = r


def _in_projection(x2d, mod4, mod_row, norm1, w_main, w_tail, rope_tabs, seq):
    m = x2d.shape[0]
    tm, tn = TOKEN_TILE, PROJ_TILE
    n_delta = N_MAIN - ATTN_COLS
    in_specs = [
        pl.BlockSpec((tm, D_MODEL), lambda i, j: (i, 0)),
        pl.BlockSpec((None, None, 1, D_MODEL), lambda i, j: (mod_row(i), 0, 0, 0)),
        pl.BlockSpec((None, None, 1, D_MODEL), lambda i, j: (mod_row(i), 1, 0, 0)),
        pl.BlockSpec((1, D_MODEL), lambda i, j: (0, 0)),
        pl.BlockSpec((D_MODEL, tn), lambda i, j: (0, j)),
        pl.BlockSpec((D_MODEL, LANES), lambda i, j: (0, 0)),
    ]
    args = [x2d, mod4, mod4, norm1, w_main, w_tail]
    main_spec = pl.BlockSpec((tm, tn), lambda i, j: (i, jnp.maximum(j - ATTN_TILES, 0)))
    main_shape = jax.ShapeDtypeStruct((m, n_delta), F32)
    tail_spec = pl.BlockSpec((tm, LANES), lambda i, j: (i, 0))
    tail_shape = jax.ShapeDtypeStruct((m, LANES), F32)
    q_tiles = ATTN_TILES // 3
    heads_per_tile = tn // LANES

    def cols_spec(first_tile):
        return pl.BlockSpec((tm, tn), lambda i, j: (i, jnp.clip(j - first_tile, 0, q_tiles - 1)))

    cols_shape = jax.ShapeDtypeStruct((m, ATTN_COLS // 3), BF16)
    if rope_tabs is not None:
        assert tm == seq
        for t in rope_tabs:
            in_specs.append(pl.BlockSpec((tm, LANES), lambda i, j: (0, 0)))
            args.append(t)
        body = _inproj_latent_kernel
        kt_spec = pl.BlockSpec((None, heads_per_tile, LANES, seq),
                               lambda i, j: (i, jnp.clip(j - q_tiles, 0, q_tiles - 1), 0, 0))
        out_specs = [cols_spec(0), kt_spec, cols_spec(2 * q_tiles), main_spec, tail_spec]
        out_shape = [cols_shape, jax.ShapeDtypeStruct((m // seq, H_A, LANES, seq), BF16), cols_shape,
                     main_shape, tail_shape]
    else:
        bt = tm // seq

        def head_spec(first_tile):
            return pl.BlockSpec((bt, heads_per_tile, seq, LANES),
                                lambda i, j: (i, jnp.clip(j - first_tile, 0, q_tiles - 1), 0, 0))

        head_shape = jax.ShapeDtypeStruct((m // seq, H_A, seq, LANES), F32)
        body = functools.partial(_inproj_context_kernel, seq=seq)
        out_specs = [cols_spec(0), head_spec(q_tiles), head_spec(2 * q_tiles), main_spec, tail_spec]
        out_shape = [cols_shape, head_shape, head_shape, main_shape, tail_shape]
    return pl.pallas_call(
        body,
        grid=(m // tm, N_MAIN // tn),
        in_specs=in_specs,
        out_specs=out_specs,
        out_shape=out_shape,
        scratch_shapes=[pltpu.VMEM((tm, D_MODEL), BF16)],
        compiler_params=_cparams(("parallel", "arbitrary")),
        name="in_projection",
    )(*args)


def _rope_tables(seq_len):
    lane = jnp.arange(LANES)
    r = lane % HD_A
    axis = r // (HD_A // 2)
    u = r % (HD_A // 2)
    nf = HD_A // 4
    f = u % nf
    hi_half = u // nf
    inv = 1.0 / (ROPE_BASE ** (f.astype(F32) * 2.0 / (HD_A // 2)))
    t = jnp.arange(seq_len)
    pos = jnp.where(axis[None, :] == 0, (t // GRID_W)[:, None], (t % GRID_W)[:, None]).astype(F32)
    ang = pos * inv[None, :]
    cos, sin = jnp.cos(ang), jnp.sin(ang)
    sin_lo = jnp.where(hi_half[None, :] == 0, -sin, 0.0)
    sin_hi = jnp.where(hi_half[None, :] == 1, sin, 0.0)
    return cos, sin_lo, sin_hi


def _attn_kernel(*refs, has_ctx):
    if has_ctx:
        q_ref, k_ref, v_ref, ck_ref, cv_ref, lam_ref, g_ref, o_ref = refs
    else:
        q_ref, k_ref, v_ref, lam_ref, g_ref, o_ref = refs
    lp = lam_ref[...]
    lam = (jnp.exp(jnp.sum(lp[0:1] * lp[1:2], axis=-1, keepdims=True))
           - jnp.exp(jnp.sum(lp[2:3] * lp[3:4], axis=-1, keepdims=True)) + LAM_INIT)
    first = (lax.broadcasted_iota(jnp.int32, (1, LANES), 1) < HD_A).astype(F32)
    keep = (first.astype(BF16), (1.0 - first).astype(BF16))
    if has_ctx:
        keys_t = [ck_ref[...], k_ref[...]]
        vals = [cv_ref[...], v_ref[...]]
    else:
        keys_t = [k_ref[...].T.astype(BF16)]
        vals = [v_ref[...].astype(BF16)]
    q = q_ref[...]
    probs = []
    for km in keep:
        s = [_dot(q * km, kt) for kt in keys_t]
        mx = functools.reduce(jnp.maximum, [jnp.max(x, axis=-1, keepdims=True) for x in s])
        e = [jnp.exp2(x - mx) for x in s]
        z = functools.reduce(jnp.add, [jnp.sum(x, axis=-1, keepdims=True) for x in e])
        probs.append((e, 1.0 / z))
    (e0, r0), (e1, r1) = probs
    r1 = r1 * lam
    o = None
    for a, b, vv in zip(e0, e1, vals):
        pd = (a * r0 - b * r1).astype(BF16)
        t = _dot(pd, vv)
        o = t if o is None else o + t
    ms = jnp.mean(o * o, axis=-1, keepdims=True)
    o = o * lax.rsqrt(ms + EPS) * g_ref[...] * (1.0 - LAM_INIT)
    o_ref[...] = o.astype(o_ref.dtype)


def _diff_attention(q3, k_src, v_src, ctx_kt, ctx_v, attn_lambda, subln, tq):
    bsz, seq, _ = q3.shape
    has_ctx = ctx_kt is not None
    head_major = pl.BlockSpec((None, None, seq, LANES), lambda b, h, i: (b, h, 0, 0))
    if has_ctx:
        k_spec = pl.BlockSpec((None, None, LANES, seq), lambda b, h, i: (b, h, 0, 0))
        v_spec = pl.BlockSpec((None, seq, LANES), lambda b, h, i: (b, 0, h))
    else:
        k_spec = v_spec = head_major
    in_specs = [pl.BlockSpec((None, tq, LANES), lambda b, h, i: (b, i, h)), k_spec, v_spec]
    args = [q3, k_src, v_src]
    if has_ctx:
        past = ctx_v.shape[2]
        in_specs += [pl.BlockSpec((None, None, LANES, past), lambda b, h, i: (b, h, 0, 0)),
                     pl.BlockSpec((None, None, past, LANES), lambda b, h, i: (b, h, 0, 0))]
        args += [ctx_kt, ctx_v]
    in_specs += [pl.BlockSpec((4, HD_A), lambda b, h, i: (0, 0)),
                 pl.BlockSpec((1, DV_A), lambda b, h, i: (0, 0))]
    args += [attn_lambda, subln]
    return pl.pallas_call(
        functools.partial(_attn_kernel, has_ctx=has_ctx),
        grid=(bsz, H_A, seq // tq),
        in_specs=in_specs,
        out_specs=pl.BlockSpec((None, tq, LANES), lambda b, h, i: (b, i, h)),
        out_shape=jax.ShapeDtypeStruct((bsz, seq, H_A * DV_A), BF16),
        compiler_params=_cparams(("parallel", "parallel", "arbitrary")),
        name="diff_attention",
    )(*args)


GATE_G, GATE_BETA, GATE_TOT = 0, 2 * H_B, 4 * H_B
NEG_BIG = -1e30


def _gates_kernel(ab_ref, alog_ref, dtb_ref, o_ref, *, seq):
    n_chunks = seq // CHUNK
    ab = ab_ref[...]
    lane = lax.broadcasted_iota(jnp.int32, ab.shape, 1)
    pos = lax.broadcasted_iota(jnp.int32, ab.shape, 0) % CHUNK
    g = -jnp.exp(alog_ref[...]) * _softplus(ab + dtb_ref[...])
    g = jnp.where(lane < 2 * H_B, g, 0.0)
    fwd = g
    bwd = g
    sft = 1
    while sft < CHUNK:
        fwd = fwd + jnp.where(pos >= sft, pltpu.roll(fwd, sft, 0), 0.0)
        bwd = bwd + jnp.where(pos < CHUNK - sft, pltpu.roll(bwd, seq - sft, 0), 0.0)
        sft *= 2
    cs = jnp.where(lane < H_B, fwd, bwd)
    tot = jnp.sum(g.reshape(n_chunks, CHUNK, LANES), axis=1, keepdims=True)
    tot = jnp.broadcast_to(tot, (n_chunks, CHUNK, LANES)).reshape(seq, LANES)
    tot = pltpu.roll(tot, GATE_TOT, 1)
    out = jnp.where(lane < GATE_BETA, cs, jnp.where(lane < GATE_TOT, _sigmoid(ab), tot))
    o_ref[...] = out


def _delta_gates(tail3, alog_row, dtb_row):
    bsz, seq, _ = tail3.shape
    return pl.pallas_call(
        functools.partial(_gates_kernel, seq=seq),
        grid=(bsz,),
        in_specs=[pl.BlockSpec((None, seq, LANES), lambda b: (b, 0, 0)),
                  pl.BlockSpec((1, LANES), lambda b: (0, 0)),
                  pl.BlockSpec((1, LANES), lambda b: (0, 0))],
        out_specs=pl.BlockSpec((None, seq, LANES), lambda b: (b, 0, 0)),
        out_shape=jax.ShapeDtypeStruct((bsz, seq, LANES), F32),
        compiler_params=_cparams(("parallel",)),
        name="delta_gates",
    )(tail3, alog_row, dtb_row)


PAD_ROWS = 8


def _conv_silu(x_ref, cols, w, pad_ref):
    seq = x_ref.shape[0]
    pad_ref[PAD_ROWS:PAD_ROWS + seq, :] = x_ref[:, cols]
    y = None
    for t in range(CONV_K):
        s = PAD_ROWS + t - CONV_K // 2
        term = pad_ref[s:s + seq, :] * w[t:t + 1]
        y = term if y is None else y + term
    return _silu(y)


def _lane_col(x, col):
    lane = lax.broadcasted_iota(jnp.int32, x.shape, 1)
    c = jnp.sum(jnp.where(lane == col, x, 0.0), axis=-1, keepdims=True)
    return jnp.broadcast_to(c, x.shape)


def _delta_kernel(*refs, seq, hps, group, has_s0, emit_state):
    it = iter(refs)
    qr_ref, kr_ref, vr_ref, gate_ref, tok_ref = (next(it) for _ in range(5))
    wq_ref, wk_ref, wv_ref, nw_ref = (next(it) for _ in range(4))
    s0_ref = next(it) if has_s0 else None
    o_ref = next(it)
    st_ref = next(it) if emit_state else None
    kq_s, rhs_s, qg_s, akg_s, g_s, beta_s, egl_s, lhs_s, add_s, oacc_s, s_s, pad_s = it

    hg = pl.program_id(1)
    n_chunks = seq // CHUNK
    n_inst = hps * n_chunks
    left = lax.broadcasted_iota(jnp.int32, (seq, LANES), 1) < CHUNK
    tok = tok_ref[...]
    pad_s[0:PAD_ROWS, :] = jnp.zeros((PAD_ROWS, LANES), F32)
    pad_s[PAD_ROWS + seq:2 * PAD_ROWS + seq, :] = jnp.zeros((PAD_ROWS, LANES), F32)

    for hh in range(hps):
        cols = slice(hh * LANES, (hh + 1) * LANES)
        head = hg * hps + hh
        inst = slice(hh * n_chunks, (hh + 1) * n_chunks)
        rows = slice(hh * seq, (hh + 1) * seq)
        q = _conv_silu(qr_ref, cols, wq_ref[:, cols], pad_s)
        k = _conv_silu(kr_ref, cols, wk_ref[:, cols], pad_s)
        v = _conv_silu(vr_ref, cols, wv_ref[:, cols], pad_s)
        q = q * lax.rsqrt(jnp.sum(q * q, axis=-1, keepdims=True) + EPS) * (DK_B ** -0.5)
        k = k * lax.rsqrt(jnp.sum(k * k, axis=-1, keepdims=True) + EPS)
        kq_s[inst, 0:CHUNK, :] = k.astype(BF16).reshape(n_chunks, CHUNK, LANES)
        kq_s[inst, CHUNK:2 * CHUNK, :] = q.astype(BF16).reshape(n_chunks, CHUNK, LANES)
        packed = []
        for d in range(2):
            cs = _lane_col(tok, GATE_G + d * H_B + head)
            beta = _lane_col(tok, GATE_BETA + d * H_B + head)
            tot = _lane_col(tok, GATE_TOT + d * H_B + head)
            eg = jnp.exp(cs)
            drows = slice(d * CHUNK, (d + 1) * CHUNK)
            rhs_s[inst, drows, 0:LANES] = (v * beta).astype(BF16).reshape(n_chunks, CHUNK, LANES)
            rhs_s[inst, drows, LANES:2 * LANES] = (k * (beta * eg)).astype(BF16).reshape(n_chunks, CHUNK, LANES)
            qg_s[d, inst] = (q * eg).reshape(n_chunks, CHUNK, LANES)
            kg = (k * jnp.exp(tot - cs)).reshape(n_chunks, CHUNK, LANES)
            packed.append((cs, beta, kg))
            egl_s[d, inst] = jnp.exp(tot.reshape(n_chunks, CHUNK, LANES)[:, 0:8, :])
        g_s[rows, :] = jnp.where(left, packed[0][0], packed[1][0])
        beta_s[rows, :] = jnp.where(left, packed[0][1], packed[1][1])
        for c in range(n_chunks):
            kg_pair = jnp.concatenate([packed[0][2][c], packed[1][2][c]], axis=0)
            akg_s[hh * n_chunks + c, CHUNK:CHUNK + DK_B, :] = kg_pair.T.astype(BF16)

    ri = lax.broadcasted_iota(jnp.int32, (CHUNK, LANES), 0)
    cj = lax.broadcasted_iota(jnp.int32, (CHUNK, LANES), 1)
    left2 = cj < CHUNK
    cj = jnp.where(left2, cj, cj - CHUNK)
    diag2 = (ri == cj)[None]
    eye2 = (ri == cj).astype(F32)[None]
    incl2 = (jnp.where(left2, ri - cj, cj - ri) >= 0)[None]
    keep_l = left2[0:1].astype(F32).astype(BF16)[None]
    keep_r = (1.0 - left2[0:1].astype(F32)).astype(BF16)[None]

    def blockdiag(x):
        return jnp.concatenate([x * keep_l, x * keep_r], axis=1)

    for i0 in range(0, n_inst, group):
        inst = slice(i0, i0 + group)
        rows = slice(i0 * CHUNK, (i0 + group) * CHUNK)
        kq = kq_s[inst]
        kc = kq[:, 0:CHUNK, :]
        kkqk = _bmm_nt(kq, jnp.concatenate([kc, kc], axis=1))
        kk, qk = kkqk[:, 0:CHUNK, :], kkqk[:, CHUNK:2 * CHUNK, :]
        gcol = g_s[rows, :].reshape(group, CHUNK, LANES)
        bcol = beta_s[rows, :].reshape(group, CHUNK, LANES)
        grow = jnp.sum(eye2 * gcol, axis=1, keepdims=True)
        decay = jnp.exp(jnp.where(incl2, gcol - grow, NEG_BIG))
        akg_s[inst, 0:CHUNK, :] = (qk * decay).astype(BF16)
        nm = jnp.where(diag2, 0.0, -(bcol * kk * decay))
        t = eye2 + nm
        nb = nm.astype(BF16)
        p = _bmm(nb, blockdiag(nb))
        for _ in range(4):
            pb = p.astype(BF16)
            tp = _bmm(jnp.concatenate([t.astype(BF16), pb], axis=1), blockdiag(pb))
            t = t + tp[:, 0:CHUNK, :]
            p = tp[:, CHUNK:2 * CHUNK, :]
        t = t + _bmm(t.astype(BF16), blockdiag(p.astype(BF16)))
        tb = t.astype(BF16)
        resid = (eye2 - t) + _bmm(nb, blockdiag(tb))
        t = t + _bmm(tb, blockdiag(resid.astype(BF16)))
        uw = _bmm(blockdiag(t.astype(BF16)), rhs_s[inst])
        uwb = uw.astype(BF16)
        akg = akg_s[inst]
        for d, keep in enumerate((keep_l, keep_r)):
            res = _bmm(akg * keep, uwb)
            lhs_s[d, inst, 0:CHUNK, :] = (qg_s[d, inst] - res[:, 0:CHUNK, LANES:2 * LANES]).astype(BF16)
            lhs_s[d, inst, CHUNK:CHUNK + DK_B, :] = (-res[:, CHUNK:CHUNK + DK_B, LANES:2 * LANES]).astype(BF16)
            add_s[d, inst] = res[:, :, 0:LANES]

    for hh in range(hps):
        for d in range(2):
            s_s[d, hh] = s0_ref[d, hh] if has_s0 else jnp.zeros((DK_B, DV_B), F32)

    def scan_step(t, carry):
        for hh in range(hps):
            for d in range(2):
                c = hh * n_chunks + (t if d == 0 else n_chunks - 1 - t)
                rows = pl.ds(pl.multiple_of(c * CHUNK, CHUNK), CHUNK)
                s = s_s[d, hh]
                r = _dot(lhs_s[d, c], s.astype(BF16)) + add_s[d, c]
                oacc_s[d, rows, :] = r[0:CHUNK]
                s_s[d, hh] = egl_s[d, c][0:1, :] * s + r[CHUNK:CHUNK + DK_B]
        return carry

    lax.fori_loop(0, n_chunks, scan_step, 0)

    for hh in range(hps):
        cols = slice(hh * LANES, (hh + 1) * LANES)
        rows = slice(hh * seq, (hh + 1) * seq)
        if emit_state:
            st_ref[0, hh] = s_s[0, hh]
            st_ref[1, hh] = s_s[1, hh]
        o = oacc_s[0, rows, :] + oacc_s[1, rows, :]
        ms = jnp.mean(o * o, axis=-1, keepdims=True)
        o = o * lax.rsqrt(ms + EPS) * nw_ref[...] * _silu(gate_ref[:, cols])
        o_ref[:, cols] = o.astype(o_ref.dtype)


def _delta_mixer(proj3, tok3, conv_w, norm_w, s0, emit_state):
    bsz, seq, _ = proj3.shape
    hps = max(2, min(H_B, 2048 // seq))
    has_s0 = s0 is not None
    n_chunks = seq // CHUNK
    n_inst = hps * n_chunks
    group = min(n_inst, 16)
    wide = hps * LANES

    def col(cb):
        return pl.BlockSpec((None, seq, wide), lambda b, g: (b, 0, cb // hps + g))

    def wcol(cb):
        return pl.BlockSpec((CONV_K, wide), lambda b, g: (0, cb // hps + g))

    in_specs = [col(COL_DQ), col(COL_DK), col(COL_DV), col(COL_GATE),
                pl.BlockSpec((None, seq, LANES), lambda b, g: (b, 0, 0)),
                wcol(0), wcol(H_B), wcol(2 * H_B),
                pl.BlockSpec((1, DV_B), lambda b, g: (0, 0))]
    args = [proj3, proj3, proj3, proj3, tok3, conv_w, conv_w, conv_w, norm_w]
    state_spec = pl.BlockSpec((None, 2, hps, DK_B, DV_B), lambda b, g: (b, 0, g, 0, 0))
    if has_s0:
        in_specs.append(state_spec)
        args.append(s0)
    out_specs = [pl.BlockSpec((None, seq, wide), lambda b, g: (b, 0, g))]
    out_shape = [jax.ShapeDtypeStruct((bsz, seq, H_B * DV_B), BF16)]
    if emit_state:
        out_specs.append(state_spec)
        out_shape.append(jax.ShapeDtypeStruct((bsz, 2, H_B, DK_B, DV_B), F32))
    scratch = [
        pltpu.VMEM((n_inst, 2 * CHUNK, LANES), BF16),
        pltpu.VMEM((n_inst, 2 * CHUNK, 2 * LANES), BF16),
        pltpu.VMEM((2, n_inst, CHUNK, LANES), F32),
        pltpu.VMEM((n_inst, CHUNK + DK_B, LANES), BF16),
        pltpu.VMEM((hps * seq, LANES), F32),
        pltpu.VMEM((hps * seq, LANES), F32),
        pltpu.VMEM((2, n_inst, 8, LANES), F32),
        pltpu.VMEM((2, n_inst, CHUNK + DK_B, LANES), BF16),
        pltpu.VMEM((2, n_inst, CHUNK + DK_B, LANES), F32),
        pltpu.VMEM((2, hps * seq, LANES), F32),
        pltpu.VMEM((2, hps, DK_B, DV_B), F32),
        pltpu.VMEM((seq + 2 * PAD_ROWS, LANES), F32),
    ]
    outs = pl.pallas_call(
        functools.partial(_delta_kernel, seq=seq, hps=hps, group=group, has_s0=has_s0,
                          emit_state=emit_state),
        grid=(bsz, H_B // hps),
        in_specs=in_specs,
        out_specs=out_specs,
        out_shape=out_shape,
        scratch_shapes=scratch,
        compiler_params=_cparams(("parallel", "parallel")),
        name="delta_mixer",
    )(*args)
    return outs if emit_state else (outs[0], None)


def _outproj_kernel(x_ref, oa_ref, ob_ref, g1_ref, wa_ref, wb_ref, h_ref):
    mixed = _dot(oa_ref[...], wa_ref[...]) + _dot(ob_ref[...], wb_ref[...])
    h_ref[...] = x_ref[...] + g1_ref[...] * mixed


def _out_projection(x2d, oa2d, ob2d, mod4, mod_row, w_out, tn=1024):
    m = x2d.shape[0]
    tm = TOKEN_TILE
    half = w_out.shape[0] // 2
    return pl.pallas_call(
        _outproj_kernel,
        grid=(m // tm, D_MODEL // tn),
        in_specs=[pl.BlockSpec((tm, tn), lambda i, j: (i, j)),
                  pl.BlockSpec((tm, half), lambda i, j: (i, 0)),
                  pl.BlockSpec((tm, half), lambda i, j: (i, 0)),
                  pl.BlockSpec((None, None, 1, tn), lambda i, j: (mod_row(i), 2, 0, j)),
                  pl.BlockSpec((half, tn), lambda i, j: (0, j)),
                  pl.BlockSpec((half, tn), lambda i, j: (1, j))],
        out_specs=pl.BlockSpec((tm, tn), lambda i, j: (i, j)),
        out_shape=jax.ShapeDtypeStruct((m, D_MODEL), F32),
        compiler_params=_cparams(("parallel", "arbitrary")),
        name="out_projection",
    )(x2d, oa2d, ob2d, mod4, w_out, w_out)


def _mlp_kernel(h_ref, sh_ref, sc_ref, g2_ref, n2_ref, nf_ref, w1_ref, w2_ref, y_ref, hn_ref):
    f = pl.program_id(1)

    @pl.when(f == 0)
    def _():
        h = h_ref[...]
        ms = jnp.mean(h * h, axis=-1, keepdims=True)
        y = h * lax.rsqrt(ms + EPS) * n2_ref[...]
        hn_ref[...] = (y * (1.0 + sc_ref[...]) + sh_ref[...]).astype(BF16)
        y_ref[...] = jnp.zeros_like(y_ref)

    a = jnp.maximum(_dot(hn_ref[...], w1_ref[...]), 0.0)
    a2 = (a * a).astype(BF16)
    for n0 in range(0, D_MODEL, MLP_OUT_CHUNK):
        sl = slice(n0, n0 + MLP_OUT_CHUNK)
        y_ref[:, sl] += _dot(a2, w2_ref[:, sl])

    @pl.when(f == pl.num_programs(1) - 1)
    def _():
        y = h_ref[...] + g2_ref[...] * y_ref[...]
        ms = jnp.mean(y * y, axis=-1, keepdims=True)
        y_ref[...] = y * lax.rsqrt(ms + EPS) * nf_ref[...]


def _mlp(h2d, mod4, mod_row, norm2, norm_f, w1, w2, tf=512):
    m = h2d.shape[0]
    tm = TOKEN_TILE

    def mod_spec(k):
        return pl.BlockSpec((None, None, 1, D_MODEL), lambda i, f: (mod_row(i), k, 0, 0))

    return pl.pallas_call(
        _mlp_kernel,
        grid=(m // tm, D_FF // tf),
        in_specs=[pl.BlockSpec((tm, D_MODEL), lambda i, f: (i, 0), pipeline_mode=pl.Buffered(1)),
                  mod_spec(3), mod_spec(4), mod_spec(5),
                  pl.BlockSpec((1, D_MODEL), lambda i, f: (0, 0)),
                  pl.BlockSpec((1, D_MODEL), lambda i, f: (0, 0)),
                  pl.BlockSpec((D_MODEL, tf), lambda i, f: (0, f)),
                  pl.BlockSpec((tf, D_MODEL), lambda i, f: (f, 0))],
        out_specs=pl.BlockSpec((tm, D_MODEL), lambda i, f: (i, 0)),
        out_shape=jax.ShapeDtypeStruct((m, D_MODEL), F32),
        scratch_shapes=[pltpu.VMEM((tm, D_MODEL), BF16)],
        compiler_params=_cparams(("parallel", "arbitrary")),
        name="mlp",
    )(h2d, mod4, mod4, mod4, norm2, norm_f, w1, w2)


def _trunk(x, mod4, mod_row, params, ctx):
    (w_main, w_tail, conv_w, alog_row, dtb_row, delta_norm, attn_lambda, attn_subln, norm1,
     w_out, norm2, w_ff1, w_ff2, norm_f) = params
    bsz, seq, _ = x.shape
    x2d = x.reshape(bsz * seq, D_MODEL)
    if ctx is None:
        q, new_k, new_v, main, tail = _in_projection(x2d, mod4, mod_row, norm1, w_main, w_tail, None, seq)
        oa = _diff_attention(q.reshape(bsz, seq, -1), new_k, new_v, None, None, attn_lambda, attn_subln,
                             tq=seq)
    else:
        ctx_kt, ctx_v, s0 = ctx
        new_k = new_v = None
        q, k_t, v, main, tail = _in_projection(x2d, mod4, mod_row, norm1, w_main, w_tail,
                                               _rope_tables(seq), seq)
        oa = _diff_attention(q.reshape(bsz, seq, -1), k_t, v.reshape(bsz, seq, -1), ctx_kt, ctx_v,
                             attn_lambda, attn_subln, tq=ATTN_Q_TILE)
    main3 = main.reshape(bsz, seq, N_MAIN - ATTN_COLS)
    tok3 = _delta_gates(tail.reshape(bsz, seq, LANES), alog_row, dtb_row)
    ob, state = _delta_mixer(main3, tok3, conv_w, delta_norm, None if ctx is None else s0, ctx is None)
    h = _out_projection(x2d, oa.reshape(bsz * seq, -1), ob.reshape(bsz * seq, -1), mod4, mod_row, w_out)
    y = _mlp(h, mod4, mod_row, norm2, norm_f, w_ff1, w_ff2)
    return y.reshape(bsz, seq, D_MODEL), new_k, new_v, state


def kernel(x_prompt, x_sample, cache_attn_k, cache_attn_v, state_delta, c, c_ctx, w_mod, b_mod, norm1, w_in, conv_w, a_log, dt_bias, delta_norm, attn_lambda, attn_subln, w_out, norm2, w_ff1, w_ff2, norm_f):
    bp, lp, _ = x_prompt.shape
    bs, ls, _ = x_sample.shape
    past = cache_attn_k.shape[3]

    rows = 16
    cc = jnp.concatenate([c_ctx[None, :], c, jnp.zeros((rows - 1 - bs, D_MODEL), F32)], axis=0)
    mod = _modulation(cc, w_mod[0], b_mod[0][None, :])
    mod4 = mod.reshape(rows, N_MOD, 1, D_MODEL)

    w_main = w_in[0].astype(BF16)
    w_tail = jnp.pad(w_in[0][:, N_MAIN:], ((0, 0), (0, LANES - N_TAIL))).astype(BF16)
    alog_row = jnp.pad(a_log[0].reshape(1, 2 * H_B), ((0, 0), (0, LANES - 2 * H_B)))
    dtb_row = jnp.pad(dt_bias[0].reshape(1, 2 * H_B), ((0, 0), (0, LANES - 2 * H_B)))
    params = (w_main, w_tail, conv_w[0], alog_row, dtb_row, delta_norm[0][None, :], attn_lambda[0],
              attn_subln[0][None, :], norm1[0][None, :], w_out[0].astype(BF16), norm2[0][None, :],
              w_ff1[0].astype(BF16), w_ff2[0].astype(BF16), norm_f[None, :])

    y_prompt, new_k, new_v, state = _trunk(x_prompt, mod4, lambda i: 0, params, None)
    ctx_kt = cache_attn_k[:, 0].reshape(bs, H_A, past, 2 * HD_A).astype(BF16).transpose(0, 1, 3, 2)
    ctx = (ctx_kt, cache_attn_v[:, 0].astype(BF16), state_delta[:, 0])
    y_sample, _, _, _ = _trunk(x_sample, mod4, lambda i: 1 + i // (ls // TOKEN_TILE), params, ctx)

    new_k = new_k.reshape(bp, 1, H_A, lp, 2, HD_A)
    new_v = new_v.reshape(bp, 1, H_A, lp, DV_A)
    return (y_prompt, y_sample, new_k, new_v, state[:, None])
```

```python
import functools
import math

import jax
import jax.numpy as jnp
from jax import lax
from jax.experimental import pallas as pl
from jax.experimental.pallas import tpu as pltpu

F32 = jnp.float32
BF16 = jnp.bfloat16

D_MODEL = 2048
GRID_W = 64
H_A = 8
HD_A = 64
DV_A = 128
H_B = 8
DK_B = 128
DV_B = 128
CONV_K = 5
CHUNK = 64
D_FF = 4 * D_MODEL
ROPE_BASE = 10000.0
EPS = 1e-6
N_MOD = 6
LAM_INIT = 0.8 - 0.6 * math.exp(-0.3 * 0)

LANES = 128
N_MAIN = 7168
N_TAIL = 32
ATTN_COLS = 3 * H_A * DV_A
PROJ_TILE = 512
ATTN_TILES = ATTN_COLS // PROJ_TILE
Q_SCALE = HD_A ** -0.5 * math.log2(math.e)
COL_DQ, COL_DK, COL_DV, COL_GATE = 0, 8, 16, 24
VMEM_LIMIT = 56 * 1024 * 1024
TOKEN_TILE = 1024
OUTPROJ_TILE = 512
MLP_OUT_CHUNK = 512
ATTN_Q_TILE = 256


def _cparams(sem):
    return pltpu.CompilerParams(dimension_semantics=sem, vmem_limit_bytes=VMEM_LIMIT)


def _sigmoid(x):
    return 0.5 * jnp.tanh(0.5 * x) + 0.5


def _silu(x):
    half = 0.5 * x
    return half * jnp.tanh(half) + half


def _softplus(x):
    return jnp.maximum(x, 0.0) + jnp.log(1.0 + jnp.exp(-jnp.abs(x)))


def _dot(a, b):
    return jnp.dot(a, b, preferred_element_type=F32)


def _bmm(a, b):
    return lax.dot_general(a, b, (((2,), (1,)), ((0,), (0,))), preferred_element_type=F32)


def _bmm_nt(a, b):
    return lax.dot_general(a, b, (((2,), (2,)), ((0,), (0,))), preferred_element_type=F32)


def _mod_kernel(c_ref, w_ref, b_ref, o_ref):
    s = _silu(c_ref[...]).astype(BF16)
    o_ref[...] = _dot(s, w_ref[...].astype(BF16)) + b_ref[...]


def _modulation(cc, w_mod, b_mod):
    rows = cc.shape[0]
    n = w_mod.shape[1]
    tn = 512
    return pl.pallas_call(
        _mod_kernel,
        grid=(n // tn,),
        in_specs=[pl.BlockSpec((rows, D_MODEL), lambda j: (0, 0)),
                  pl.BlockSpec((D_MODEL, tn), lambda j: (0, j)),
                  pl.BlockSpec((1, tn), lambda j: (0, j))],
        out_specs=pl.BlockSpec((rows, tn), lambda j: (0, j)),
        out_shape=jax.ShapeDtypeStruct((rows, n), F32),
        compiler_params=_cparams(("arbitrary",)),
        name="modulation",
    )(cc, w_mod, b_mod)


def _rope(x, cos, sin_lo, sin_hi):
    return (x * cos + pltpu.roll(x, LANES - 16, 1) * sin_lo + pltpu.roll(x, 16, 1) * sin_hi)


def _inproj_norm(x_ref, sh_ref, sc_ref, n1_ref, wt_ref, ot_ref, xn_ref):
    x = x_ref[...]
    ms = jnp.mean(x * x, axis=-1, keepdims=True)
    y = x * lax.rsqrt(ms + EPS) * n1_ref[...]
    y = y * (1.0 + sc_ref[...]) + sh_ref[...]
    xn = y.astype(BF16)
    xn_ref[...] = xn
    ot_ref[...] = _dot(xn, wt_ref[...])


def _inproj_latent_kernel(x_ref, sh_ref, sc_ref, n1_ref, w_ref, wt_ref, cos_ref, slo_ref, shi_ref,
                          q_ref, kt_ref, v_ref, main_ref, ot_ref, xn_ref):
    j = pl.program_id(1)
    q_tiles = ATTN_TILES // 3
    heads = [slice(g * LANES, (g + 1) * LANES) for g in range(PROJ_TILE // LANES)]

    @pl.when(j == 0)
    def _():
        _inproj_norm(x_ref, sh_ref, sc_ref, n1_ref, wt_ref, ot_ref, xn_ref)

    r = _dot(xn_ref[...], w_ref[...])

    @pl.when(j < q_tiles)
    def _():
        cos, slo, shi = cos_ref[...], slo_ref[...], shi_ref[...]
        for sl in heads:
            q_ref[:, sl] = (_rope(r[:, sl], cos, slo, shi) * Q_SCALE).astype(BF16)

    @pl.when((j >= q_tiles) & (j < 2 * q_tiles))
    def _():
        cos, slo, shi = cos_ref[...], slo_ref[...], shi_ref[...]
        for g, sl in enumerate(heads):
            kt_ref[g] = _rope(r[:, sl], cos, slo, shi).T.astype(BF16)

    @pl.when((j >= 2 * q_tiles) & (j < ATTN_TILES))
    def _():
        v_ref[...] = r.astype(BF16)

    @pl.when(j >= ATTN_TILES)
    def _():
        main_ref[...] = r


def _inproj_context_kernel(x_ref, sh_ref, sc_ref, n1_ref, w_ref, wt_ref,
                           q_ref, k_ref, v_ref, main_ref, ot_ref, xn_ref, *, seq):
    j = pl.program_id(1)
    q_tiles = ATTN_TILES // 3

    @pl.when(j == 0)
    def _():
        _inproj_norm(x_ref, sh_ref, sc_ref, n1_ref, wt_ref, ot_ref, xn_ref)

    r = _dot(xn_ref[...], w_ref[...])

    def store_heads(dst_ref):
        for g in range(PROJ_TILE // LANES):
            dst_ref[:, g] = r[:, g * LANES:(g + 1) * LANES].reshape(-1, seq, LANES)

    @pl.when(j < q_tiles)
    def _():
        q_ref[...] = (r * Q_SCALE).astype(BF16)

    @pl.when((j >= q_tiles) & (j < 2 * q_tiles))
    def _():
        store_heads(k_ref)

    @pl.when((j >= 2 * q_tiles) & (j < ATTN_TILES))
    def _():
        store_heads(v_ref)

    @pl.when(j >= ATTN_TILES)
    def _():
        main_ref[...] = r


def _in_projection(x2d, mod4, mod_row, norm1, w_main, w_tail, rope_tabs, seq):
    m = x2d.shape[0]
    tm, tn = TOKEN_TILE, PROJ_TILE
    n_delta = N_MAIN - ATTN_COLS
    in_specs = [
        pl.BlockSpec((tm, D_MODEL), lambda i, j: (i, 0)),
        pl.BlockSpec((None, None, 1, D_MODEL), lambda i, j: (mod_row(i, tm), 0, 0, 0)),
        pl.BlockSpec((None, None, 1, D_MODEL), lambda i, j: (mod_row(i, tm), 1, 0, 0)),
        pl.BlockSpec((1, D_MODEL), lambda i, j: (0, 0)),
        pl.BlockSpec((D_MODEL, tn), lambda i, j: (0, j)),
        pl.BlockSpec((D_MODEL, LANES), lambda i, j: (0, 0)),
    ]
    args = [x2d, mod4, mod4, norm1, w_main, w_tail]
    main_spec = pl.BlockSpec((tm, tn), lambda i, j: (i, jnp.maximum(j - ATTN_TILES, 0)))
    main_shape = jax.ShapeDtypeStruct((m, n_delta), F32)
    tail_spec = pl.BlockSpec((tm, LANES), lambda i, j: (i, 0))
    tail_shape = jax.ShapeDtypeStruct((m, LANES), F32)
    q_tiles = ATTN_TILES // 3
    heads_per_tile = tn // LANES

    def cols_spec(first_tile):
        return pl.BlockSpec((tm, tn), lambda i, j: (i, jnp.clip(j - first_tile, 0, q_tiles - 1)))

    cols_shape = jax.ShapeDtypeStruct((m, ATTN_COLS // 3), BF16)
    if rope_tabs is not None:
        assert tm == seq
        for t in rope_tabs:
            in_specs.append(pl.BlockSpec((tm, LANES), lambda i, j: (0, 0)))
            args.append(t)
        body = _inproj_latent_kernel
        kt_spec = pl.BlockSpec((None, heads_per_tile, LANES, seq),
                               lambda i, j: (i, jnp.clip(j - q_tiles, 0, q_tiles - 1), 0, 0))
        out_specs = [cols_spec(0), kt_spec, cols_spec(2 * q_tiles), main_spec, tail_spec]
        out_shape = [cols_shape, jax.ShapeDtypeStruct((m // seq, H_A, LANES, seq), BF16), cols_shape,
                     main_shape, tail_shape]
    else:
        bt = tm // seq

        def head_spec(first_tile):
            return pl.BlockSpec((bt, heads_per_tile, seq, LANES),
                                lambda i, j: (i, jnp.clip(j - first_tile, 0, q_tiles - 1), 0, 0))

        head_shape = jax.ShapeDtypeStruct((m // seq, H_A, seq, LANES), F32)
        body = functools.partial(_inproj_context_kernel, seq=seq)
        out_specs = [cols_spec(0), head_spec(q_tiles), head_spec(2 * q_tiles), main_spec, tail_spec]
        out_shape = [cols_shape, head_shape, head_shape, main_shape, tail_shape]
    return pl.pallas_call(
        body,
        grid=(m // tm, N_MAIN // tn),
        in_specs=in_specs,
        out_specs=out_specs,
        out_shape=out_shape,
        scratch_shapes=[pltpu.VMEM((tm, D_MODEL), BF16)],
        compiler_params=_cparams(("parallel", "arbitrary")),
        name="in_projection",
    )(*args)


def _rope_tables(seq_len):
    lane = jnp.arange(LANES)
    r = lane % HD_A
    axis = r // (HD_A // 2)
    u = r % (HD_A // 2)
    nf = HD_A // 4
    f = u % nf
    hi_half = u // nf
    inv = 1.0 / (ROPE_BASE ** (f.astype(F32) * 2.0 / (HD_A // 2)))
    t = jnp.arange(seq_len)
    pos = jnp.where(axis[None, :] == 0, (t // GRID_W)[:, None], (t % GRID_W)[:, None]).astype(F32)
    ang = pos * inv[None, :]
    cos, sin = jnp.cos(ang), jnp.sin(ang)
    sin_lo = jnp.where(hi_half[None, :] == 0, -sin, 0.0)
    sin_hi = jnp.where(hi_half[None, :] == 1, sin, 0.0)
    return cos, sin_lo, sin_hi


def _attn_kernel(*refs, has_ctx):
    if has_ctx:
        q_ref, k_ref, v_ref, ck_ref, cv_ref, lam_ref, g_ref, o_ref = refs
    else:
        q_ref, k_ref, v_ref, lam_ref, g_ref, o_ref = refs
    lp = lam_ref[...]
    lam = (jnp.exp(jnp.sum(lp[0:1] * lp[1:2], axis=-1, keepdims=True))
           - jnp.exp(jnp.sum(lp[2:3] * lp[3:4], axis=-1, keepdims=True)) + LAM_INIT)
    first = (lax.broadcasted_iota(jnp.int32, (1, LANES), 1) < HD_A).astype(F32)
    keep = (first.astype(BF16), (1.0 - first).astype(BF16))
    if has_ctx:
        keys_t = [ck_ref[...], k_ref[...]]
        vals = [cv_ref[...], v_ref[...]]
    else:
        keys_t = [k_ref[...].T.astype(BF16)]
        vals = [v_ref[...].astype(BF16)]
    q = q_ref[...]
    probs = []
    for km in keep:
        s = [_dot(q * km, kt) for kt in keys_t]
        mx = functools.reduce(jnp.maximum, [jnp.max(x, axis=-1, keepdims=True) for x in s])
        e = [jnp.exp2(x - mx) for x in s]
        z = functools.reduce(jnp.add, [jnp.sum(x, axis=-1, keepdims=True) for x in e])
        probs.append((e, 1.0 / z))
    (e0, r0), (e1, r1) = probs
    r1 = r1 * lam
    o = None
    for a, b, vv in zip(e0, e1, vals):
        pd = (a * r0 - b * r1).astype(BF16)
        t = _dot(pd, vv)
        o = t if o is None else o + t
    ms = jnp.mean(o * o, axis=-1, keepdims=True)
    o = o * lax.rsqrt(ms + EPS) * g_ref[...] * (1.0 - LAM_INIT)
    o_ref[...] = o.astype(o_ref.dtype)


def _diff_attention(q3, k_src, v_src, ctx_kt, ctx_v, attn_lambda, subln, tq):
    bsz, seq, _ = q3.shape
    has_ctx = ctx_kt is not None
    head_major = pl.BlockSpec((None, None, seq, LANES), lambda b, h, i: (b, h, 0, 0))
    if has_ctx:
        k_spec = pl.BlockSpec((None, None, LANES, seq), lambda b, h, i: (b, h, 0, 0))
        v_spec = pl.BlockSpec((None, seq, LANES), lambda b, h, i: (b, 0, h))
    else:
        k_spec = v_spec = head_major
    in_specs = [pl.BlockSpec((None, tq, LANES), lambda b, h, i: (b, i, h)), k_spec, v_spec]
    args = [q3, k_src, v_src]
    if has_ctx:
        past = ctx_v.shape[2]
        in_specs += [pl.BlockSpec((None, None, LANES, past), lambda b, h, i: (b, h, 0, 0)),
                     pl.BlockSpec((None, None, past, LANES), lambda b, h, i: (b, h, 0, 0))]
        args += [ctx_kt, ctx_v]
    in_specs += [pl.BlockSpec((4, HD_A), lambda b, h, i: (0, 0)),
                 pl.BlockSpec((1, DV_A), lambda b, h, i: (0, 0))]
    args += [attn_lambda, subln]
    return pl.pallas_call(
        functools.partial(_attn_kernel, has_ctx=has_ctx),
        grid=(bsz, H_A, seq // tq),
        in_specs=in_specs,
        out_specs=pl.BlockSpec((None, tq, LANES), lambda b, h, i: (b, i, h)),
        out_shape=jax.ShapeDtypeStruct((bsz, seq, H_A * DV_A), BF16),
        compiler_params=_cparams(("parallel", "parallel", "arbitrary")),
        name="diff_attention",
    )(*args)


GATE_G, GATE_BETA, GATE_TOT = 0, 2 * H_B, 4 * H_B
NEG_BIG = -1e30


def _gates_kernel(ab_ref, alog_ref, dtb_ref, o_ref, *, seq):
    n_chunks = seq // CHUNK
    ab = ab_ref[...]
    lane = lax.broadcasted_iota(jnp.int32, ab.shape, 1)
    pos = lax.broadcasted_iota(jnp.int32, ab.shape, 0) % CHUNK
    g = -jnp.exp(alog_ref[...]) * _softplus(ab + dtb_ref[...])
    g = jnp.where(lane < 2 * H_B, g, 0.0)
    fwd = g
    bwd = g
    sft = 1
    while sft < CHUNK:
        fwd = fwd + jnp.where(pos >= sft, pltpu.roll(fwd, sft, 0), 0.0)
        bwd = bwd + jnp.where(pos < CHUNK - sft, pltpu.roll(bwd, seq - sft, 0), 0.0)
        sft *= 2
    cs = jnp.where(lane < H_B, fwd, bwd)
    tot = jnp.sum(g.reshape(n_chunks, CHUNK, LANES), axis=1, keepdims=True)
    tot = jnp.broadcast_to(tot, (n_chunks, CHUNK, LANES)).reshape(seq, LANES)
    tot = pltpu.roll(tot, GATE_TOT, 1)
    out = jnp.where(lane < GATE_BETA, cs, jnp.where(lane < GATE_TOT, _sigmoid(ab), tot))
    o_ref[...] = out


def _delta_gates(tail3, alog_row, dtb_row):
    bsz, seq, _ = tail3.shape
    return pl.pallas_call(
        functools.partial(_gates_kernel, seq=seq),
        grid=(bsz,),
        in_specs=[pl.BlockSpec((None, seq, LANES), lambda b: (b, 0, 0)),
                  pl.BlockSpec((1, LANES), lambda b: (0, 0)),
                  pl.BlockSpec((1, LANES), lambda b: (0, 0))],
        out_specs=pl.BlockSpec((None, seq, LANES), lambda b: (b, 0, 0)),
        out_shape=jax.ShapeDtypeStruct((bsz, seq, LANES), F32),
        compiler_params=_cparams(("parallel",)),
        name="delta_gates",
    )(tail3, alog_row, dtb_row)


PAD_ROWS = 8


def _conv_silu(x_ref, cols, w, pad_ref):
    seq = x_ref.shape[0]
    pad_ref[PAD_ROWS:PAD_ROWS + seq, :] = x_ref[:, cols]
    y = None
    for t in range(CONV_K):
        s = PAD_ROWS + t - CONV_K // 2
        term = pad_ref[s:s + seq, :] * w[t:t + 1]
        y = term if y is None else y + term
    return _silu(y)


def _lane_col(x, col):
    lane = lax.broadcasted_iota(jnp.int32, x.shape, 1)
    c = jnp.sum(jnp.where(lane == col, x, 0.0), axis=-1, keepdims=True)
    return jnp.broadcast_to(c, x.shape)


def _delta_kernel(*refs, seq, hps, group, has_s0, emit_state):
    it = iter(refs)
    qr_ref, kr_ref, vr_ref, gate_ref, tok_ref = (next(it) for _ in range(5))
    wq_ref, wk_ref, wv_ref, nw_ref = (next(it) for _ in range(4))
    s0_ref = next(it) if has_s0 else None
    o_ref = next(it)
    st_ref = next(it) if emit_state else None
    kq_s, rhs_s, qg_s, akg_s, g_s, beta_s, egl_s, lhs_s, add_s, oacc_s, s_s, pad_s = it

    hg = pl.program_id(1)
    n_chunks = seq // CHUNK
    n_inst = hps * n_chunks
    left = lax.broadcasted_iota(jnp.int32, (seq, LANES), 1) < CHUNK
    tok = tok_ref[...]
    pad_s[0:PAD_ROWS, :] = jnp.zeros((PAD_ROWS, LANES), F32)
    pad_s[PAD_ROWS + seq:2 * PAD_ROWS + seq, :] = jnp.zeros((PAD_ROWS, LANES), F32)

    for hh in range(hps):
        cols = slice(hh * LANES, (hh + 1) * LANES)
        head = hg * hps + hh
        inst = slice(hh * n_chunks, (hh + 1) * n_chunks)
        rows = slice(hh * seq, (hh + 1) * seq)
        q = _conv_silu(qr_ref, cols, wq_ref[:, cols], pad_s)
        k = _conv_silu(kr_ref, cols, wk_ref[:, cols], pad_s)
        v = _conv_silu(vr_ref, cols, wv_ref[:, cols], pad_s)
        q = q * lax.rsqrt(jnp.sum(q * q, axis=-1, keepdims=True) + EPS) * (DK_B ** -0.5)
        k = k * lax.rsqrt(jnp.sum(k * k, axis=-1, keepdims=True) + EPS)
        kq_s[inst, 0:CHUNK, :] = k.astype(BF16).reshape(n_chunks, CHUNK, LANES)
        kq_s[inst, CHUNK:2 * CHUNK, :] = q.astype(BF16).reshape(n_chunks, CHUNK, LANES)
        packed = []
        for d in range(2):
            cs = _lane_col(tok, GATE_G + d * H_B + head)
            beta = _lane_col(tok, GATE_BETA + d * H_B + head)
            tot = _lane_col(tok, GATE_TOT + d * H_B + head)
            eg = jnp.exp(cs)
            drows = slice(d * CHUNK, (d + 1) * CHUNK)
            rhs_s[inst, drows, 0:LANES] = (v * beta).astype(BF16).reshape(n_chunks, CHUNK, LANES)
            rhs_s[inst, drows, LANES:2 * LANES] = (k * (beta * eg)).astype(BF16).reshape(n_chunks, CHUNK, LANES)
            qg_s[d, inst] = (q * eg).reshape(n_chunks, CHUNK, LANES)
            kg = (k * jnp.exp(tot - cs)).reshape(n_chunks, CHUNK, LANES)
            packed.append((cs, beta, kg))
            egl_s[d, inst] = jnp.exp(tot.reshape(n_chunks, CHUNK, LANES)[:, 0:8, :])
        g_s[rows, :] = jnp.where(left, packed[0][0], packed[1][0])
        beta_s[rows, :] = jnp.where(left, packed[0][1], packed[1][1])
        for c in range(n_chunks):
            kg_pair = jnp.concatenate([packed[0][2][c], packed[1][2][c]], axis=0)
            akg_s[hh * n_chunks + c, CHUNK:CHUNK + DK_B, :] = kg_pair.T.astype(BF16)

    ri = lax.broadcasted_iota(jnp.int32, (CHUNK, LANES), 0)
    cj = lax.broadcasted_iota(jnp.int32, (CHUNK, LANES), 1)
    left2 = cj < CHUNK
    cj = jnp.where(left2, cj, cj - CHUNK)
    diag2 = (ri == cj)[None]
    eye2 = (ri == cj).astype(F32)[None]
    incl2 = (jnp.where(left2, ri - cj, cj - ri) >= 0)[None]
    keep_l = left2[0:1].astype(F32).astype(BF16)[None]
    keep_r = (1.0 - left2[0:1].astype(F32)).astype(BF16)[None]

    def blockdiag(x):
        return jnp.concatenate([x * keep_l, x * keep_r], axis=1)

    for i0 in range(0, n_inst, group):
        inst = slice(i0, i0 + group)
        rows = slice(i0 * CHUNK, (i0 + group) * CHUNK)
        kq = kq_s[inst]
        kc = kq[:, 0:CHUNK, :]
        kkqk = _bmm_nt(kq, jnp.concatenate([kc, kc], axis=1))
        kk, qk = kkqk[:, 0:CHUNK, :], kkqk[:, CHUNK:2 * CHUNK, :]
        gcol = g_s[rows, :].reshape(group, CHUNK, LANES)
        bcol = beta_s[rows, :].reshape(group, CHUNK, LANES)
        grow = jnp.sum(eye2 * gcol, axis=1, keepdims=True)
        decay = jnp.exp(jnp.where(incl2, gcol - grow, NEG_BIG))
        akg_s[inst, 0:CHUNK, :] = (qk * decay).astype(BF16)
        nm = jnp.where(diag2, 0.0, -(bcol * kk * decay))
        t = eye2 + nm
        nb = nm.astype(BF16)
        p = _bmm(nb, blockdiag(nb))
        for _ in range(4):
            pb = p.astype(BF16)
            tp = _bmm(jnp.concatenate([t.astype(BF16), pb], axis=1), blockdiag(pb))
            t = t + tp[:, 0:CHUNK, :]
            p = tp[:, CHUNK:2 * CHUNK, :]
        t = t + _bmm(t.astype(BF16), blockdiag(p.astype(BF16)))
        tb = t.astype(BF16)
        resid = (eye2 - t) + _bmm(nb, blockdiag(tb))
        t = t + _bmm(tb, blockdiag(resid.astype(BF16)))
        uw = _bmm(blockdiag(t.astype(BF16)), rhs_s[inst])
        uwb = uw.astype(BF16)
        akg = akg_s[inst]
        for d, keep in enumerate((keep_l, keep_r)):
            res = _bmm(akg * keep, uwb)
            lhs_s[d, inst, 0:CHUNK, :] = (qg_s[d, inst] - res[:, 0:CHUNK, LANES:2 * LANES]).astype(BF16)
            lhs_s[d, inst, CHUNK:CHUNK + DK_B, :] = (-res[:, CHUNK:CHUNK + DK_B, LANES:2 * LANES]).astype(BF16)
            add_s[d, inst] = res[:, :, 0:LANES]

    for hh in range(hps):
        for d in range(2):
            s_s[d, hh] = s0_ref[d, hh] if has_s0 else jnp.zeros((DK_B, DV_B), F32)

    def scan_step(t, carry):
        for hh in range(hps):
            for d in range(2):
                c = hh * n_chunks + (t if d == 0 else n_chunks - 1 - t)
                rows = pl.ds(pl.multiple_of(c * CHUNK, CHUNK), CHUNK)
                s = s_s[d, hh]
                r = _dot(lhs_s[d, c], s.astype(BF16)) + add_s[d, c]
                oacc_s[d, rows, :] = r[0:CHUNK]
                s_s[d, hh] = egl_s[d, c][0:1, :] * s + r[CHUNK:CHUNK + DK_B]
        return carry

    lax.fori_loop(0, n_chunks, scan_step, 0)

    for hh in range(hps):
        cols = slice(hh * LANES, (hh + 1) * LANES)
        rows = slice(hh * seq, (hh + 1) * seq)
        if emit_state:
            st_ref[0, hh] = s_s[0, hh]
            st_ref[1, hh] = s_s[1, hh]
        o = oacc_s[0, rows, :] + oacc_s[1, rows, :]
        ms = jnp.mean(o * o, axis=-1, keepdims=True)
        o = o * lax.rsqrt(ms + EPS) * nw_ref[...] * _silu(gate_ref[:, cols])
        o_ref[:, cols] = o.astype(o_ref.dtype)


def _delta_mixer(proj3, tok3, conv_w, norm_w, s0, emit_state):
    bsz, seq, _ = proj3.shape
    hps = max(2, min(H_B, 2048 // seq))
    has_s0 = s0 is not None
    n_chunks = seq // CHUNK
    n_inst = hps * n_chunks
    group = min(n_inst, 16)
    wide = hps * LANES

    def col(cb):
        return pl.BlockSpec((None, seq, wide), lambda b, g: (b, 0, cb // hps + g))

    def wcol(cb):
        return pl.BlockSpec((CONV_K, wide), lambda b, g: (0, cb // hps + g))

    in_specs = [col(COL_DQ), col(COL_DK), col(COL_DV), col(COL_GATE),
                pl.BlockSpec((None, seq, LANES), lambda b, g: (b, 0, 0)),
                wcol(0), wcol(H_B), wcol(2 * H_B),
                pl.BlockSpec((1, DV_B), lambda b, g: (0, 0))]
    args = [proj3, proj3, proj3, proj3, tok3, conv_w, conv_w, conv_w, norm_w]
    state_spec = pl.BlockSpec((None, 2, hps, DK_B, DV_B), lambda b, g: (b, 0, g, 0, 0))
    if has_s0:
        in_specs.append(state_spec)
        args.append(s0)
    out_specs = [pl.BlockSpec((None, seq, wide), lambda b, g: (b, 0, g))]
    out_shape = [jax.ShapeDtypeStruct((bsz, seq, H_B * DV_B), BF16)]
    if emit_state:
        out_specs.append(state_spec)
        out_shape.append(jax.ShapeDtypeStruct((bsz, 2, H_B, DK_B, DV_B), F32))
    scratch = [
        pltpu.VMEM((n_inst, 2 * CHUNK, LANES), BF16),
        pltpu.VMEM((n_inst, 2 * CHUNK, 2 * LANES), BF16),
        pltpu.VMEM((2, n_inst, CHUNK, LANES), F32),
        pltpu.VMEM((n_inst, CHUNK + DK_B, LANES), BF16),
        pltpu.VMEM((hps * seq, LANES), F32),
        pltpu.VMEM((hps * seq, LANES), F32),
        pltpu.VMEM((2, n_inst, 8, LANES), F32),
        pltpu.VMEM((2, n_inst, CHUNK + DK_B, LANES), BF16),
        pltpu.VMEM((2, n_inst, CHUNK + DK_B, LANES), F32),
        pltpu.VMEM((2, hps * seq, LANES), F32),
        pltpu.VMEM((2, hps, DK_B, DV_B), F32),
        pltpu.VMEM((seq + 2 * PAD_ROWS, LANES), F32),
    ]
    outs = pl.pallas_call(
        functools.partial(_delta_kernel, seq=seq, hps=hps, group=group, has_s0=has_s0,
                          emit_state=emit_state),
        grid=(bsz, H_B // hps),
        in_specs=in_specs,
        out_specs=out_specs,
        out_shape=out_shape,
        scratch_shapes=scratch,
        compiler_params=_cparams(("parallel", "parallel")),
        name="delta_mixer",
    )(*args)
    return outs if emit_state else (outs[0], None)


def _outproj_kernel(x_ref, oa_ref, ob_ref, g1_ref, sh_ref, sc_ref, n2_ref, wa_ref, wb_ref, h_ref, hn_ref):
    mixed = _dot(oa_ref[...], wa_ref[...]) + _dot(ob_ref[...], wb_ref[...])
    h = x_ref[...] + g1_ref[...] * mixed
    h_ref[...] = h
    ms = jnp.mean(h * h, axis=-1, keepdims=True)
    y = h * lax.rsqrt(ms + EPS) * n2_ref[...]
    hn_ref[...] = (y * (1.0 + sc_ref[...]) + sh_ref[...]).astype(BF16)


def _out_projection(x2d, oa2d, ob2d, mod4, mod_row, norm2, w_out):
    m = x2d.shape[0]
    tm = OUTPROJ_TILE
    half = w_out.shape[0] // 2

    def mod_spec(k):
        return pl.BlockSpec((None, None, 1, D_MODEL), lambda i: (mod_row(i, tm), k, 0, 0))

    def weight_spec(part):
        return pl.BlockSpec((half, D_MODEL), lambda i: (part, 0), pipeline_mode=pl.Buffered(1))

    return pl.pallas_call(
        _outproj_kernel,
        grid=(m // tm,),
        in_specs=[pl.BlockSpec((tm, D_MODEL), lambda i: (i, 0)),
                  pl.BlockSpec((tm, half), lambda i: (i, 0)),
                  pl.BlockSpec((tm, half), lambda i: (i, 0)),
                  mod_spec(2), mod_spec(3), mod_spec(4),
                  pl.BlockSpec((1, D_MODEL), lambda i: (0, 0)),
                  weight_spec(0), weight_spec(1)],
        out_specs=[pl.BlockSpec((tm, D_MODEL), lambda i: (i, 0)),
                   pl.BlockSpec((tm, D_MODEL), lambda i: (i, 0))],
        out_shape=[jax.ShapeDtypeStruct((m, D_MODEL), F32),
                   jax.ShapeDtypeStruct((m, D_MODEL), BF16)],
        compiler_params=_cparams(("parallel",)),
        name="out_projection",
    )(x2d, oa2d, ob2d, mod4, mod4, mod4, norm2, w_out, w_out)


def _mlp_kernel(hn_ref, h_ref, g2_ref, nf_ref, w1_ref, w2_ref, y_ref):
    f = pl.program_id(1)

    @pl.when(f == 0)
    def _():
        y_ref[...] = jnp.zeros_like(y_ref)

    a = jnp.maximum(_dot(hn_ref[...], w1_ref[...]), 0.0)
    a2 = (a * a).astype(BF16)
    for n0 in range(0, D_MODEL, MLP_OUT_CHUNK):
        sl = slice(n0, n0 + MLP_OUT_CHUNK)
        y_ref[:, sl] += _dot(a2, w2_ref[:, sl])

    @pl.when(f == pl.num_programs(1) - 1)
    def _():
        y = h_ref[...] + g2_ref[...] * y_ref[...]
        ms = jnp.mean(y * y, axis=-1, keepdims=True)
        y_ref[...] = y * lax.rsqrt(ms + EPS) * nf_ref[...]


def _mlp(hn2d, h2d, mod4, mod_row, norm_f, w1, w2, tf=512):
    m = h2d.shape[0]
    tm = TOKEN_TILE
    row_tile = pl.BlockSpec((tm, D_MODEL), lambda i, f: (i, 0), pipeline_mode=pl.Buffered(1))
    return pl.pallas_call(
        _mlp_kernel,
        grid=(m // tm, D_FF // tf),
        in_specs=[row_tile, row_tile,
                  pl.BlockSpec((None, None, 1, D_MODEL), lambda i, f: (mod_row(i, tm), 5, 0, 0)),
                  pl.BlockSpec((1, D_MODEL), lambda i, f: (0, 0)),
                  pl.BlockSpec((D_MODEL, tf), lambda i, f: (0, f)),
                  pl.BlockSpec((tf, D_MODEL), lambda i, f: (f, 0))],
        out_specs=pl.BlockSpec((tm, D_MODEL), lambda i, f: (i, 0)),
        out_shape=jax.ShapeDtypeStruct((m, D_MODEL), F32),
        compiler_params=_cparams(("parallel", "arbitrary")),
        name="mlp",
    )(hn2d, h2d, mod4, norm_f, w1, w2)


def _trunk(x, mod4, mod_row, params, ctx):
    (w_main, w_tail, conv_w, alog_row, dtb_row, delta_norm, attn_lambda, attn_subln, norm1,
     w_out, norm2, w_ff1, w_ff2, norm_f) = params
    bsz, seq, _ = x.shape
    x2d = x.reshape(bsz * seq, D_MODEL)
    if ctx is None:
        q, new_k, new_v, main, tail = _in_projection(x2d, mod4, mod_row, norm1, w_main, w_tail, None, seq)
        oa = _diff_attention(q.reshape(bsz, seq, -1), new_k, new_v, None, None, attn_lambda, attn_subln,
                             tq=seq)
    else:
        ctx_kt, ctx_v, s0 = ctx
        new_k = new_v = None
        q, k_t, v, main, tail = _in_projection(x2d, mod4, mod_row, norm1, w_main, w_tail,
                                               _rope_tables(seq), seq)
        oa = _diff_attention(q.reshape(bsz, seq, -1), k_t, v.reshape(bsz, seq, -1), ctx_kt, ctx_v,
                             attn_lambda, attn_subln, tq=ATTN_Q_TILE)
    main3 = main.reshape(bsz, seq, N_MAIN - ATTN_COLS)
    tok3 = _delta_gates(tail.reshape(bsz, seq, LANES), alog_row, dtb_row)
    ob, state = _delta_mixer(main3, tok3, conv_w, delta_norm, None if ctx is None else s0, ctx is None)
    h, hn = _out_projection(x2d, oa.reshape(bsz * seq, -1), ob.reshape(bsz * seq, -1), mod4, mod_row,
                            norm2, w_out)
    y = _mlp(hn, h, mod4, mod_row, norm_f, w_ff1, w_ff2)
    return y.reshape(bsz, seq, D_MODEL), new_k, new_v, state


def kernel(x_prompt, x_sample, cache_attn_k, cache_attn_v, state_delta, c, c_ctx, w_mod, b_mod, norm1, w_in, conv_w, a_log, dt_bias, delta_norm, attn_lambda, attn_subln, w_out, norm2, w_ff1, w_ff2, norm_f):
    bp, lp, _ = x_prompt.shape
    bs, ls, _ = x_sample.shape
    past = cache_attn_k.shape[3]

    rows = 16
    cc = jnp.concatenate([c_ctx[None, :], c, jnp.zeros((rows - 1 - bs, D_MODEL), F32)], axis=0)
    mod = _modulation(cc, w_mod[0], b_mod[0][None, :])
    mod4 = mod.reshape(rows, N_MOD, 1, D_MODEL)

    w_main = w_in[0].astype(BF16)
    w_tail = jnp.pad(w_in[0][:, N_MAIN:], ((0, 0), (0, LANES - N_TAIL))).astype(BF16)
    alog_row = jnp.pad(a_log[0].reshape(1, 2 * H_B), ((0, 0), (0, LANES - 2 * H_B)))
    dtb_row = jnp.pad(dt_bias[0].reshape(1, 2 * H_B), ((0, 0), (0, LANES - 2 * H_B)))
    params = (w_main, w_tail, conv_w[0], alog_row, dtb_row, delta_norm[0][None, :], attn_lambda[0],
              attn_subln[0][None, :], norm1[0][None, :], w_out[0].astype(BF16), norm2[0][None, :],
              w_ff1[0].astype(BF16), w_ff2[0].astype(BF16), norm_f[None, :])

    y_prompt, new_k, new_v, state = _trunk(x_prompt, mod4, lambda i, tm: 0, params, None)
    ctx_kt = cache_attn_k[:, 0].reshape(bs, H_A, past, 2 * HD_A).astype(BF16).transpose(0, 1, 3, 2)
    ctx = (ctx_kt, cache_attn_v[:, 0].astype(BF16), state_delta[:, 0])
    y_sample, _, _, _ = _trunk(x_sample, mod4, lambda i, tm: 1 + (i * tm) // ls, params, ctx)

    new_k = new_k.reshape(bp, 1, H_A, lp, 2, HD_A)
    new_v = new_v.reshape(bp, 1, H_A, lp, DV_A)
    return (y_prompt, y_sample, new_k, new_v, state[:, None])
```

```python
import functools
import math

import jax
import jax.numpy as jnp
from jax import lax
from jax.experimental import pallas as pl
from jax.experimental.pallas import tpu as pltpu

F32 = jnp.float32
BF16 = jnp.bfloat16

D_MODEL = 2048
GRID_W = 64
H_A = 8
HD_A = 64
DV_A = 128
H_B = 8
DK_B = 128
DV_B = 128
CONV_K = 5
CHUNK = 64
D_FF = 4 * D_MODEL
ROPE_BASE = 10000.0
EPS = 1e-6
N_MOD = 6
LAM_INIT = 0.8 - 0.6 * math.exp(-0.3 * 0)

LANES = 128
N_MAIN = 7168
N_TAIL = 32
ATTN_COLS = 3 * H_A * DV_A
PROJ_TILE = 512
ATTN_TILES = ATTN_COLS // PROJ_TILE
Q_SCALE = HD_A ** -0.5 * math.log2(math.e)
COL_DQ, COL_DK, COL_DV, COL_GATE = 0, 8, 16, 24
VMEM_LIMIT = 56 * 1024 * 1024
TOKEN_TILE = 1024
OUTPROJ_TILE = 512
MLP_OUT_CHUNK = 512
ATTN_Q_TILE = 1024
ATTN_ROW_BLOCK = 256


def _cparams(sem):
    return pltpu.CompilerParams(dimension_semantics=sem, vmem_limit_bytes=VMEM_LIMIT)


def _sigmoid(x):
    return 0.5 * jnp.tanh(0.5 * x) + 0.5


def _silu(x):
    half = 0.5 * x
    return half * jnp.tanh(half) + half


def _softplus(x):
    return jnp.maximum(x, 0.0) + jnp.log(1.0 + jnp.exp(-jnp.abs(x)))


def _dot(a, b):
    return jnp.dot(a, b, preferred_element_type=F32)


def _bmm(a, b):
    return lax.dot_general(a, b, (((2,), (1,)), ((0,), (0,))), preferred_element_type=F32)


def _bmm_nt(a, b):
    return lax.dot_general(a, b, (((2,), (2,)), ((0,), (0,))), preferred_element_type=F32)


def _mod_kernel(c_ref, w_ref, b_ref, o_ref):
    s = _silu(c_ref[...]).astype(BF16)
    o_ref[...] = _dot(s, w_ref[...].astype(BF16)) + b_ref[...]


def _modulation(cc, w_mod, b_mod):
    rows = cc.shape[0]
    n = w_mod.shape[1]
    tn = 512
    return pl.pallas_call(
        _mod_kernel,
        grid=(n // tn,),
        in_specs=[pl.BlockSpec((rows, D_MODEL), lambda j: (0, 0)),
                  pl.BlockSpec((D_MODEL, tn), lambda j: (0, j)),
                  pl.BlockSpec((1, tn), lambda j: (0, j))],
        out_specs=pl.BlockSpec((rows, tn), lambda j: (0, j)),
        out_shape=jax.ShapeDtypeStruct((rows, n), F32),
        compiler_params=_cparams(("arbitrary",)),
        name="modulation",
    )(cc, w_mod, b_mod)


def _rope(x, cos, sin_lo, sin_hi):
    return (x * cos + pltpu.roll(x, LANES - 16, 1) * sin_lo + pltpu.roll(x, 16, 1) * sin_hi)


def _inproj_norm(x_ref, sh_ref, sc_ref, n1_ref, wt_ref, ot_ref, xn_ref):
    x = x_ref[...]
    ms = jnp.mean(x * x, axis=-1, keepdims=True)
    y = x * lax.rsqrt(ms + EPS) * n1_ref[...]
    y = y * (1.0 + sc_ref[...]) + sh_ref[...]
    xn = y.astype(BF16)
    xn_ref[...] = xn
    ot_ref[...] = _dot(xn, wt_ref[...])


def _inproj_latent_kernel(x_ref, sh_ref, sc_ref, n1_ref, w_ref, wt_ref, cos_ref, slo_ref, shi_ref,
                          q_ref, kt_ref, v_ref, main_ref, ot_ref, xn_ref):
    j = pl.program_id(1)
    q_tiles = ATTN_TILES // 3
    heads = [slice(g * LANES, (g + 1) * LANES) for g in range(PROJ_TILE // LANES)]

    @pl.when(j == 0)
    def _():
        _inproj_norm(x_ref, sh_ref, sc_ref, n1_ref, wt_ref, ot_ref, xn_ref)

    r = _dot(xn_ref[...], w_ref[...])

    @pl.when(j < q_tiles)
    def _():
        cos, slo, shi = cos_ref[...], slo_ref[...], shi_ref[...]
        for sl in heads:
            q_ref[:, sl] = (_rope(r[:, sl], cos, slo, shi) * Q_SCALE).astype(BF16)

    @pl.when((j >= q_tiles) & (j < 2 * q_tiles))
    def _():
        cos, slo, shi = cos_ref[...], slo_ref[...], shi_ref[...]
        for g, sl in enumerate(heads):
            kt_ref[g] = _rope(r[:, sl], cos, slo, shi).T.astype(BF16)

    @pl.when((j >= 2 * q_tiles) & (j < ATTN_TILES))
    def _():
        v_ref[...] = r.astype(BF16)

    @pl.when(j >= ATTN_TILES)
    def _():
        main_ref[...] = r


def _inproj_context_kernel(x_ref, sh_ref, sc_ref, n1_ref, w_ref, wt_ref,
                           q_ref, k_ref, v_ref, main_ref, ot_ref, xn_ref, *, seq):
    j = pl.program_id(1)
    q_tiles = ATTN_TILES // 3

    @pl.when(j == 0)
    def _():
        _inproj_norm(x_ref, sh_ref, sc_ref, n1_ref, wt_ref, ot_ref, xn_ref)

    r = _dot(xn_ref[...], w_ref[...])

    def store_heads(dst_ref):
        for g in range(PROJ_TILE // LANES):
            dst_ref[:, g] = r[:, g * LANES:(g + 1) * LANES].reshape(-1, seq, LANES)

    @pl.when(j < q_tiles)
    def _():
        q_ref[...] = (r * Q_SCALE).astype(BF16)

    @pl.when((j >= q_tiles) & (j < 2 * q_tiles))
    def _():
        store_heads(k_ref)

    @pl.when((j >= 2 * q_tiles) & (j < ATTN_TILES))
    def _():
        store_heads(v_ref)

    @pl.when(j >= ATTN_TILES)
    def _():
        main_ref[...] = r


def _in_projection(x2d, mod4, mod_row, norm1, w_main, w_tail, rope_tabs, seq):
    m = x2d.shape[0]
    tm, tn = TOKEN_TILE, PROJ_TILE
    n_delta = N_MAIN - ATTN_COLS
    in_specs = [
        pl.BlockSpec((tm, D_MODEL), lambda i, j: (i, 0)),
        pl.BlockSpec((None, None, 1, D_MODEL), lambda i, j: (mod_row(i, tm), 0, 0, 0)),
        pl.BlockSpec((None, None, 1, D_MODEL), lambda i, j: (mod_row(i, tm), 1, 0, 0)),
        pl.BlockSpec((1, D_MODEL), lambda i, j: (0, 0)),
        pl.BlockSpec((D_MODEL, tn), lambda i, j: (0, j)),
        pl.BlockSpec((D_MODEL, LANES), lambda i, j: (0, 0)),
    ]
    args = [x2d, mod4, mod4, norm1, w_main, w_tail]
    main_spec = pl.BlockSpec((tm, tn), lambda i, j: (i, jnp.maximum(j - ATTN_TILES, 0)))
    main_shape = jax.ShapeDtypeStruct((m, n_delta), F32)
    tail_spec = pl.BlockSpec((tm, LANES), lambda i, j: (i, 0))
    tail_shape = jax.ShapeDtypeStruct((m, LANES), F32)
    q_tiles = ATTN_TILES // 3
    heads_per_tile = tn // LANES

    def cols_spec(first_tile):
        return pl.BlockSpec((tm, tn), lambda i, j: (i, jnp.clip(j - first_tile, 0, q_tiles - 1)))

    cols_shape = jax.ShapeDtypeStruct((m, ATTN_COLS // 3), BF16)
    if rope_tabs is not None:
        assert tm == seq
        for t in rope_tabs:
            in_specs.append(pl.BlockSpec((tm, LANES), lambda i, j: (0, 0)))
            args.append(t)
        body = _inproj_latent_kernel
        kt_spec = pl.BlockSpec((None, heads_per_tile, LANES, seq),
                               lambda i, j: (i, jnp.clip(j - q_tiles, 0, q_tiles - 1), 0, 0))
        out_specs = [cols_spec(0), kt_spec, cols_spec(2 * q_tiles), main_spec, tail_spec]
        out_shape = [cols_shape, jax.ShapeDtypeStruct((m // seq, H_A, LANES, seq), BF16), cols_shape,
                     main_shape, tail_shape]
    else:
        bt = tm // seq

        def head_spec(first_tile):
            return pl.BlockSpec((bt, heads_per_tile, seq, LANES),
                                lambda i, j: (i, jnp.clip(j - first_tile, 0, q_tiles - 1), 0, 0))

        head_shape = jax.ShapeDtypeStruct((m // seq, H_A, seq, LANES), F32)
        body = functools.partial(_inproj_context_kernel, seq=seq)
        out_specs = [cols_spec(0), head_spec(q_tiles), head_spec(2 * q_tiles), main_spec, tail_spec]
        out_shape = [cols_shape, head_shape, head_shape, main_shape, tail_shape]
    return pl.pallas_call(
        body,
        grid=(m // tm, N_MAIN // tn),
        in_specs=in_specs,
        out_specs=out_specs,
        out_shape=out_shape,
        scratch_shapes=[pltpu.VMEM((tm, D_MODEL), BF16)],
        compiler_params=_cparams(("parallel", "arbitrary")),
        name="in_projection",
    )(*args)


def _rope_tables(seq_len):
    lane = jnp.arange(LANES)
    r = lane % HD_A
    axis = r // (HD_A // 2)
    u = r % (HD_A // 2)
    nf = HD_A // 4
    f = u % nf
    hi_half = u // nf
    inv = 1.0 / (ROPE_BASE ** (f.astype(F32) * 2.0 / (HD_A // 2)))
    t = jnp.arange(seq_len)
    pos = jnp.where(axis[None, :] == 0, (t // GRID_W)[:, None], (t % GRID_W)[:, None]).astype(F32)
    ang = pos * inv[None, :]
    cos, sin = jnp.cos(ang), jnp.sin(ang)
    sin_lo = jnp.where(hi_half[None, :] == 0, -sin, 0.0)
    sin_hi = jnp.where(hi_half[None, :] == 1, sin, 0.0)
    return cos, sin_lo, sin_hi


def _attn_kernel(*refs, has_ctx):
    if has_ctx:
        q_ref, k_ref, v_ref, ck_ref, cv_ref, lam_ref, g_ref, o_ref = refs
    else:
        q_ref, k_ref, v_ref, lam_ref, g_ref, o_ref = refs
    lp = lam_ref[...]
    lam = (jnp.exp(jnp.sum(lp[0:1] * lp[1:2], axis=-1, keepdims=True))
           - jnp.exp(jnp.sum(lp[2:3] * lp[3:4], axis=-1, keepdims=True)) + LAM_INIT)
    first = (lax.broadcasted_iota(jnp.int32, (1, LANES), 1) < HD_A).astype(F32)
    keep = (first.astype(BF16), (1.0 - first).astype(BF16))
    if has_ctx:
        keys_t = [ck_ref[...], k_ref[...]]
        vals = [cv_ref[...], v_ref[...]]
    else:
        keys_t = [k_ref[...].T.astype(BF16)]
        vals = [v_ref[...].astype(BF16)]
    blocks = [slice(r, r + ATTN_ROW_BLOCK) for r in range(0, q_ref.shape[0], ATTN_ROW_BLOCK)]

    def scores(rws):
        q = q_ref[rws, :]
        return [[_dot(q * km, kt) for kt in keys_t] for km in keep]

    def softmax_diff(sb):
        probs = []
        for s in sb:
            mx = functools.reduce(jnp.maximum, [jnp.max(x, axis=-1, keepdims=True) for x in s])
            e = [jnp.exp2(x - mx) for x in s]
            z = functools.reduce(jnp.add, [jnp.sum(x, axis=-1, keepdims=True) for x in e])
            probs.append((e, 1.0 / z))
        (e0, r0), (e1, r1) = probs
        r1 = r1 * lam
        return [(a * r0 - b * r1).astype(BF16) for a, b in zip(e0, e1)]

    def output(rws, pd):
        o = functools.reduce(jnp.add, [_dot(p, vv) for p, vv in zip(pd, vals)])
        ms = jnp.mean(o * o, axis=-1, keepdims=True)
        o = o * lax.rsqrt(ms + EPS) * g_ref[...] * (1.0 - LAM_INIT)
        o_ref[rws, :] = o.astype(o_ref.dtype)

    sc, pd = {}, {}
    for i in range(len(blocks) + 2):
        if i < len(blocks):
            sc[i] = scores(blocks[i])
        if 0 <= i - 1 < len(blocks):
            pd[i - 1] = softmax_diff(sc.pop(i - 1))
        if 0 <= i - 2 < len(blocks):
            output(blocks[i - 2], pd.pop(i - 2))


def _diff_attention(q3, k_src, v_src, ctx_kt, ctx_v, attn_lambda, subln, tq):
    bsz, seq, _ = q3.shape
    has_ctx = ctx_kt is not None
    head_major = pl.BlockSpec((None, None, seq, LANES), lambda b, h, i: (b, h, 0, 0))
    if has_ctx:
        k_spec = pl.BlockSpec((None, None, LANES, seq), lambda b, h, i: (b, h, 0, 0))
        v_spec = pl.BlockSpec((None, seq, LANES), lambda b, h, i: (b, 0, h))
    else:
        k_spec = v_spec = head_major
    in_specs = [pl.BlockSpec((None, tq, LANES), lambda b, h, i: (b, i, h)), k_spec, v_spec]
    args = [q3, k_src, v_src]
    if has_ctx:
        past = ctx_v.shape[2]
        in_specs += [pl.BlockSpec((None, None, LANES, past), lambda b, h, i: (b, h, 0, 0)),
                     pl.BlockSpec((None, None, past, LANES), lambda b, h, i: (b, h, 0, 0))]
        args += [ctx_kt, ctx_v]
    in_specs += [pl.BlockSpec((4, HD_A), lambda b, h, i: (0, 0)),
                 pl.BlockSpec((1, DV_A), lambda b, h, i: (0, 0))]
    args += [attn_lambda, subln]
    return pl.pallas_call(
        functools.partial(_attn_kernel, has_ctx=has_ctx),
        grid=(bsz, H_A, seq // tq),
        in_specs=in_specs,
        out_specs=pl.BlockSpec((None, tq, LANES), lambda b, h, i: (b, i, h)),
        out_shape=jax.ShapeDtypeStruct((bsz, seq, H_A * DV_A), BF16),
        compiler_params=_cparams(("parallel", "parallel", "arbitrary")),
        name="diff_attention",
    )(*args)


GATE_G, GATE_BETA, GATE_TOT = 0, 2 * H_B, 4 * H_B
NEG_BIG = -1e30


def _gates_kernel(ab_ref, alog_ref, dtb_ref, o_ref, *, seq):
    n_chunks = seq // CHUNK
    ab = ab_ref[...]
    lane = lax.broadcasted_iota(jnp.int32, ab.shape, 1)
    pos = lax.broadcasted_iota(jnp.int32, ab.shape, 0) % CHUNK
    g = -jnp.exp(alog_ref[...]) * _softplus(ab + dtb_ref[...])
    g = jnp.where(lane < 2 * H_B, g, 0.0)
    fwd = g
    bwd = g
    sft = 1
    while sft < CHUNK:
        fwd = fwd + jnp.where(pos >= sft, pltpu.roll(fwd, sft, 0), 0.0)
        bwd = bwd + jnp.where(pos < CHUNK - sft, pltpu.roll(bwd, seq - sft, 0), 0.0)
        sft *= 2
    cs = jnp.where(lane < H_B, fwd, bwd)
    tot = jnp.sum(g.reshape(n_chunks, CHUNK, LANES), axis=1, keepdims=True)
    tot = jnp.broadcast_to(tot, (n_chunks, CHUNK, LANES)).reshape(seq, LANES)
    tot = pltpu.roll(tot, GATE_TOT, 1)
    out = jnp.where(lane < GATE_BETA, cs, jnp.where(lane < GATE_TOT, _sigmoid(ab), tot))
    o_ref[...] = out


def _delta_gates(tail3, alog_row, dtb_row):
    bsz, seq, _ = tail3.shape
    return pl.pallas_call(
        functools.partial(_gates_kernel, seq=seq),
        grid=(bsz,),
        in_specs=[pl.BlockSpec((None, seq, LANES), lambda b: (b, 0, 0)),
                  pl.BlockSpec((1, LANES), lambda b: (0, 0)),
                  pl.BlockSpec((1, LANES), lambda b: (0, 0))],
        out_specs=pl.BlockSpec((None, seq, LANES), lambda b: (b, 0, 0)),
        out_shape=jax.ShapeDtypeStruct((bsz, seq, LANES), F32),
        compiler_params=_cparams(("parallel",)),
        name="delta_gates",
    )(tail3, alog_row, dtb_row)


PAD_ROWS = 8


def _conv_silu(x_ref, cols, w, pad_ref):
    seq = x_ref.shape[0]
    pad_ref[PAD_ROWS:PAD_ROWS + seq, :] = x_ref[:, cols]
    y = None
    for t in range(CONV_K):
        s = PAD_ROWS + t - CONV_K // 2
        term = pad_ref[s:s + seq, :] * w[t:t + 1]
        y = term if y is None else y + term
    return _silu(y)


def _lane_col(x, col):
    lane = lax.broadcasted_iota(jnp.int32, x.shape, 1)
    c = jnp.sum(jnp.where(lane == col, x, 0.0), axis=-1, keepdims=True)
    return jnp.broadcast_to(c, x.shape)


def _delta_kernel(*refs, seq, hps, group, has_s0, emit_state):
    it = iter(refs)
    qr_ref, kr_ref, vr_ref, gate_ref, tok_ref = (next(it) for _ in range(5))
    wq_ref, wk_ref, wv_ref, nw_ref = (next(it) for _ in range(4))
    s0_ref = next(it) if has_s0 else None
    o_ref = next(it)
    st_ref = next(it) if emit_state else None
    kq_s, rhs_s, qg_s, akg_s, g_s, beta_s, egl_s, lhs_s, add_s, oacc_s, s_s, pad_s = it

    hg = pl.program_id(1)
    n_chunks = seq // CHUNK
    n_inst = hps * n_chunks
    left = lax.broadcasted_iota(jnp.int32, (seq, LANES), 1) < CHUNK
    tok = tok_ref[...]
    pad_s[0:PAD_ROWS, :] = jnp.zeros((PAD_ROWS, LANES), F32)
    pad_s[PAD_ROWS + seq:2 * PAD_ROWS + seq, :] = jnp.zeros((PAD_ROWS, LANES), F32)

    def phase_a(hh):
        cols = slice(hh * LANES, (hh + 1) * LANES)
        head = hg * hps + hh
        inst = slice(hh * n_chunks, (hh + 1) * n_chunks)
        rows = slice(hh * seq, (hh + 1) * seq)
        q = _conv_silu(qr_ref, cols, wq_ref[:, cols], pad_s)
        k = _conv_silu(kr_ref, cols, wk_ref[:, cols], pad_s)
        v = _conv_silu(vr_ref, cols, wv_ref[:, cols], pad_s)
        q = q * lax.rsqrt(jnp.sum(q * q, axis=-1, keepdims=True) + EPS) * (DK_B ** -0.5)
        k = k * lax.rsqrt(jnp.sum(k * k, axis=-1, keepdims=True) + EPS)
        kq_s[inst, 0:CHUNK, :] = k.astype(BF16).reshape(n_chunks, CHUNK, LANES)
        kq_s[inst, CHUNK:2 * CHUNK, :] = q.astype(BF16).reshape(n_chunks, CHUNK, LANES)
        packed = []
        for d in range(2):
            cs = _lane_col(tok, GATE_G + d * H_B + head)
            beta = _lane_col(tok, GATE_BETA + d * H_B + head)
            tot = _lane_col(tok, GATE_TOT + d * H_B + head)
            eg = jnp.exp(cs)
            drows = slice(d * CHUNK, (d + 1) * CHUNK)
            rhs_s[inst, drows, 0:LANES] = (v * beta).astype(BF16).reshape(n_chunks, CHUNK, LANES)
            rhs_s[inst, drows, LANES:2 * LANES] = (k * (beta * eg)).astype(BF16).reshape(n_chunks, CHUNK, LANES)
            qg_s[d, inst] = (q * eg).reshape(n_chunks, CHUNK, LANES)
            kg = (k * jnp.exp(tot - cs)).reshape(n_chunks, CHUNK, LANES)
            packed.append((cs, beta, kg))
            egl_s[d, inst] = jnp.exp(tot.reshape(n_chunks, CHUNK, LANES)[:, 0:8, :])
        g_s[rows, :] = jnp.where(left, packed[0][0], packed[1][0])
        beta_s[rows, :] = jnp.where(left, packed[0][1], packed[1][1])
        for c in range(n_chunks):
            kg_pair = jnp.concatenate([packed[0][2][c], packed[1][2][c]], axis=0)
            akg_s[hh * n_chunks + c, CHUNK:CHUNK + DK_B, :] = kg_pair.T.astype(BF16)

    ri = lax.broadcasted_iota(jnp.int32, (CHUNK, LANES), 0)
    cj = lax.broadcasted_iota(jnp.int32, (CHUNK, LANES), 1)
    left2 = cj < CHUNK
    cj = jnp.where(left2, cj, cj - CHUNK)
    diag2 = (ri == cj)[None]
    eye2 = (ri == cj).astype(F32)[None]
    incl2 = (jnp.where(left2, ri - cj, cj - ri) >= 0)[None]
    keep_l = left2[0:1].astype(F32).astype(BF16)[None]
    keep_r = (1.0 - left2[0:1].astype(F32)).astype(BF16)[None]

    def blockdiag(x):
        return jnp.concatenate([x * keep_l, x * keep_r], axis=1)

    def phase_b(i0):
        inst = slice(i0, i0 + group)
        rows = slice(i0 * CHUNK, (i0 + group) * CHUNK)
        kq = kq_s[inst]
        kc = kq[:, 0:CHUNK, :]
        kkqk = _bmm_nt(kq, jnp.concatenate([kc, kc], axis=1))
        kk, qk = kkqk[:, 0:CHUNK, :], kkqk[:, CHUNK:2 * CHUNK, :]
        gcol = g_s[rows, :].reshape(group, CHUNK, LANES)
        bcol = beta_s[rows, :].reshape(group, CHUNK, LANES)
        grow = jnp.sum(eye2 * gcol, axis=1, keepdims=True)
        decay = jnp.exp(jnp.where(incl2, gcol - grow, NEG_BIG))
        akg_s[inst, 0:CHUNK, :] = (qk * decay).astype(BF16)
        nm = jnp.where(diag2, 0.0, -(bcol * kk * decay))
        t = eye2 + nm
        nb = nm.astype(BF16)
        p = _bmm(nb, blockdiag(nb))
        for _ in range(4):
            pb = p.astype(BF16)
            tp = _bmm(jnp.concatenate([t.astype(BF16), pb], axis=1), blockdiag(pb))
            t = t + tp[:, 0:CHUNK, :]
            p = tp[:, CHUNK:2 * CHUNK, :]
        t = t + _bmm(t.astype(BF16), blockdiag(p.astype(BF16)))
        tb = t.astype(BF16)
        resid = (eye2 - t) + _bmm(nb, blockdiag(tb))
        t = t + _bmm(tb, blockdiag(resid.astype(BF16)))
        uw = _bmm(blockdiag(t.astype(BF16)), rhs_s[inst])
        uwb = uw.astype(BF16)
        akg = akg_s[inst]
        for d, keep in enumerate((keep_l, keep_r)):
            res = _bmm(akg * keep, uwb)
            lhs_s[d, inst, 0:CHUNK, :] = (qg_s[d, inst] - res[:, 0:CHUNK, LANES:2 * LANES]).astype(BF16)
            lhs_s[d, inst, CHUNK:CHUNK + DK_B, :] = (-res[:, CHUNK:CHUNK + DK_B, LANES:2 * LANES]).astype(BF16)
            add_s[d, inst] = res[:, :, 0:LANES]

    heads_per_group = max(1, group // n_chunks)
    for i0 in range(0, n_inst, group):
        first = i0 // n_chunks
        for hh in range(first, min(hps, first + heads_per_group)):
            phase_a(hh)
        phase_b(i0)

    for hh in range(hps):
        for d in range(2):
            s_s[d, hh] = s0_ref[d, hh] if has_s0 else jnp.zeros((DK_B, DV_B), F32)

    def scan_step(t, carry):
        for hh in range(hps):
            for d in range(2):
                c = hh * n_chunks + (t if d == 0 else n_chunks - 1 - t)
                rows = pl.ds(pl.multiple_of(c * CHUNK, CHUNK), CHUNK)
                s = s_s[d, hh]
                r = _dot(lhs_s[d, c], s.astype(BF16)) + add_s[d, c]
                oacc_s[d, rows, :] = r[0:CHUNK]
                s_s[d, hh] = egl_s[d, c][0:1, :] * s + r[CHUNK:CHUNK + DK_B]
        return carry

    lax.fori_loop(0, n_chunks, scan_step, 0)

    for hh in range(hps):
        cols = slice(hh * LANES, (hh + 1) * LANES)
        rows = slice(hh * seq, (hh + 1) * seq)
        if emit_state:
            st_ref[0, hh] = s_s[0, hh]
            st_ref[1, hh] = s_s[1, hh]
        o = oacc_s[0, rows, :] + oacc_s[1, rows, :]
        ms = jnp.mean(o * o, axis=-1, keepdims=True)
        o = o * lax.rsqrt(ms + EPS) * nw_ref[...] * _silu(gate_ref[:, cols])
        o_ref[:, cols] = o.astype(o_ref.dtype)


def _delta_mixer(proj3, tok3, conv_w, norm_w, s0, emit_state):
    bsz, seq, _ = proj3.shape
    hps = max(2, min(H_B, 2048 // seq))
    has_s0 = s0 is not None
    n_chunks = seq // CHUNK
    n_inst = hps * n_chunks
    group = min(n_inst, 16)
    wide = hps * LANES

    def col(cb):
        return pl.BlockSpec((None, seq, wide), lambda b, g: (b, 0, cb // hps + g))

    def wcol(cb):
        return pl.BlockSpec((CONV_K, wide), lambda b, g: (0, cb // hps + g))

    in_specs = [col(COL_DQ), col(COL_DK), col(COL_DV), col(COL_GATE),
                pl.BlockSpec((None, seq, LANES), lambda b, g: (b, 0, 0)),
                wcol(0), wcol(H_B), wcol(2 * H_B),
                pl.BlockSpec((1, DV_B), lambda b, g: (0, 0))]
    args = [proj3, proj3, proj3, proj3, tok3, conv_w, conv_w, conv_w, norm_w]
    state_spec = pl.BlockSpec((None, 2, hps, DK_B, DV_B), lambda b, g: (b, 0, g, 0, 0))
    if has_s0:
        in_specs.append(state_spec)
        args.append(s0)
    out_specs = [pl.BlockSpec((None, seq, wide), lambda b, g: (b, 0, g))]
    out_shape = [jax.ShapeDtypeStruct((bsz, seq, H_B * DV_B), BF16)]
    if emit_state:
        out_specs.append(state_spec)
        out_shape.append(jax.ShapeDtypeStruct((bsz, 2, H_B, DK_B, DV_B), F32))
    scratch = [
        pltpu.VMEM((n_inst, 2 * CHUNK, LANES), BF16),
        pltpu.VMEM((n_inst, 2 * CHUNK, 2 * LANES), BF16),
        pltpu.VMEM((2, n_inst, CHUNK, LANES), F32),
        pltpu.VMEM((n_inst, CHUNK + DK_B, LANES), BF16),
        pltpu.VMEM((hps * seq, LANES), F32),
        pltpu.VMEM((hps * seq, LANES), F32),
        pltpu.VMEM((2, n_inst, 8, LANES), F32),
        pltpu.VMEM((2, n_inst, CHUNK + DK_B, LANES), BF16),
        pltpu.VMEM((2, n_inst, CHUNK + DK_B, LANES), F32),
        pltpu.VMEM((2, hps * seq, LANES), F32),
        pltpu.VMEM((2, hps, DK_B, DV_B), F32),
        pltpu.VMEM((seq + 2 * PAD_ROWS, LANES), F32),
    ]
    outs = pl.pallas_call(
        functools.partial(_delta_kernel, seq=seq, hps=hps, group=group, has_s0=has_s0,
                          emit_state=emit_state),
        grid=(bsz, H_B // hps),
        in_specs=in_specs,
        out_specs=out_specs,
        out_shape=out_shape,
        scratch_shapes=scratch,
        compiler_params=_cparams(("parallel", "parallel")),
        name="delta_mixer",
    )(*args)
    return outs if emit_state else (outs[0], None)


def _outproj_kernel(x_ref, oa_ref, ob_ref, g1_ref, sh_ref, sc_ref, n2_ref, wa_ref, wb_ref, h_ref, hn_ref):
    mixed = _dot(oa_ref[...], wa_ref[...]) + _dot(ob_ref[...], wb_ref[...])
    h = x_ref[...] + g1_ref[...] * mixed
    h_ref[...] = h
    ms = jnp.mean(h * h, axis=-1, keepdims=True)
    y = h * lax.rsqrt(ms + EPS) * n2_ref[...]
    hn_ref[...] = (y * (1.0 + sc_ref[...]) + sh_ref[...]).astype(BF16)


def _out_projection(x2d, oa2d, ob2d, mod4, mod_row, norm2, w_out):
    m = x2d.shape[0]
    tm = OUTPROJ_TILE
    half = w_out.shape[0] // 2

    def mod_spec(k):
        return pl.BlockSpec((None, None, 1, D_MODEL), lambda i: (mod_row(i, tm), k, 0, 0))

    def weight_spec(part):
        return pl.BlockSpec((half, D_MODEL), lambda i: (part, 0), pipeline_mode=pl.Buffered(1))

    return pl.pallas_call(
        _outproj_kernel,
        grid=(m // tm,),
        in_specs=[pl.BlockSpec((tm, D_MODEL), lambda i: (i, 0)),
                  pl.BlockSpec((tm, half), lambda i: (i, 0)),
                  pl.BlockSpec((tm, half), lambda i: (i, 0)),
                  mod_spec(2), mod_spec(3), mod_spec(4),
                  pl.BlockSpec((1, D_MODEL), lambda i: (0, 0)),
                  weight_spec(0), weight_spec(1)],
        out_specs=[pl.BlockSpec((tm, D_MODEL), lambda i: (i, 0)),
                   pl.BlockSpec((tm, D_MODEL), lambda i: (i, 0))],
        out_shape=[jax.ShapeDtypeStruct((m, D_MODEL), F32),
                   jax.ShapeDtypeStruct((m, D_MODEL), BF16)],
        compiler_params=_cparams(("parallel",)),
        name="out_projection",
    )(x2d, oa2d, ob2d, mod4, mod4, mod4, norm2, w_out, w_out)


def _mlp_kernel(hn_ref, h_ref, g2_ref, nf_ref, w1_ref, w2_ref, y_ref):
    f = pl.program_id(1)

    @pl.when(f == 0)
    def _():
        y_ref[...] = jnp.zeros_like(y_ref)

    a = jnp.maximum(_dot(hn_ref[...], w1_ref[...]), 0.0)
    a2 = (a * a).astype(BF16)
    for n0 in range(0, D_MODEL, MLP_OUT_CHUNK):
        sl = slice(n0, n0 + MLP_OUT_CHUNK)
        y_ref[:, sl] += _dot(a2, w2_ref[:, sl])

    @pl.when(f == pl.num_programs(1) - 1)
    def _():
        y = h_ref[...] + g2_ref[...] * y_ref[...]
        ms = jnp.mean(y * y, axis=-1, keepdims=True)
        y_ref[...] = y * lax.rsqrt(ms + EPS) * nf_ref[...]


def _mlp(hn2d, h2d, mod4, mod_row, norm_f, w1, w2, tf=512):
    m = h2d.shape[0]
    tm = TOKEN_TILE
    row_tile = pl.BlockSpec((tm, D_MODEL), lambda i, f: (i, 0), pipeline_mode=pl.Buffered(1))
    return pl.pallas_call(
        _mlp_kernel,
        grid=(m // tm, D_FF // tf),
        in_specs=[row_tile, row_tile,
                  pl.BlockSpec((None, None, 1, D_MODEL), lambda i, f: (mod_row(i, tm), 5, 0, 0)),
                  pl.BlockSpec((1, D_MODEL), lambda i, f: (0, 0)),
                  pl.BlockSpec((D_MODEL, tf), lambda i, f: (0, f)),
                  pl.BlockSpec((tf, D_MODEL), lambda i, f: (f, 0))],
        out_specs=pl.BlockSpec((tm, D_MODEL), lambda i, f: (i, 0)),
        out_shape=jax.ShapeDtypeStruct((m, D_MODEL), F32),
        compiler_params=_cparams(("parallel", "arbitrary")),
        name="mlp",
    )(hn2d, h2d, mod4, norm_f, w1, w2)


def _trunk(x, mod4, mod_row, params, ctx):
    (w_main, w_tail, conv_w, alog_row, dtb_row, delta_norm, attn_lambda, attn_subln, norm1,
     w_out, norm2, w_ff1, w_ff2, norm_f) = params
    bsz, seq, _ = x.shape
    x2d = x.reshape(bsz * seq, D_MODEL)
    if ctx is None:
        q, new_k, new_v, main, tail = _in_projection(x2d, mod4, mod_row, norm1, w_main, w_tail, None, seq)
        oa = _diff_attention(q.reshape(bsz, seq, -1), new_k, new_v, None, None, attn_lambda, attn_subln,
                             tq=seq)
    else:
        ctx_kt, ctx_v, s0 = ctx
        new_k = new_v = None
        q, k_t, v, main, tail = _in_projection(x2d, mod4, mod_row, norm1, w_main, w_tail,
                                               _rope_tables(seq), seq)
        oa = _diff_attention(q.reshape(bsz, seq, -1), k_t, v.reshape(bsz, seq, -1), ctx_kt, ctx_v,
                             attn_lambda, attn_subln, tq=ATTN_Q_TILE)
    main3 = main.reshape(bsz, seq, N_MAIN - ATTN_COLS)
    tok3 = _delta_gates(tail.reshape(bsz, seq, LANES), alog_row, dtb_row)
    ob, state = _delta_mixer(main3, tok3, conv_w, delta_norm, None if ctx is None else s0, ctx is None)
    h, hn = _out_projection(x2d, oa.reshape(bsz * seq, -1), ob.reshape(bsz * seq, -1), mod4, mod_row,
                            norm2, w_out)
    y = _mlp(hn, h, mod4, mod_row, norm_f, w_ff1, w_ff2)
    return y.reshape(bsz, seq, D_MODEL), new_k, new_v, state


def kernel(x_prompt, x_sample, cache_attn_k, cache_attn_v, state_delta, c, c_ctx, w_mod, b_mod, norm1, w_in, conv_w, a_log, dt_bias, delta_norm, attn_lambda, attn_subln, w_out, norm2, w_ff1, w_ff2, norm_f):
    bp, lp, _ = x_prompt.shape
    bs, ls, _ = x_sample.shape
    past = cache_attn_k.shape[3]

    rows = 16
    cc = jnp.concatenate([c_ctx[None, :], c, jnp.zeros((rows - 1 - bs, D_MODEL), F32)], axis=0)
    mod = _modulation(cc, w_mod[0], b_mod[0][None, :])
    mod4 = mod.reshape(rows, N_MOD, 1, D_MODEL)

    w_main = w_in[0].astype(BF16)
    w_tail = jnp.pad(w_in[0][:, N_MAIN:], ((0, 0), (0, LANES - N_TAIL))).astype(BF16)
    alog_row = jnp.pad(a_log[0].reshape(1, 2 * H_B), ((0, 0), (0, LANES - 2 * H_B)))
    dtb_row = jnp.pad(dt_bias[0].reshape(1, 2 * H_B), ((0, 0), (0, LANES - 2 * H_B)))
    params = (w_main, w_tail, conv_w[0], alog_row, dtb_row, delta_norm[0][None, :], attn_lambda[0],
              attn_subln[0][None, :], norm1[0][None, :], w_out[0].astype(BF16), norm2[0][None, :],
              w_ff1[0].astype(BF16), w_ff2[0].astype(BF16), norm_f[None, :])

    y_prompt, new_k, new_v, state = _trunk(x_prompt, mod4, lambda i, tm: 0, params, None)
    ctx_kt = cache_attn_k[:, 0].reshape(bs, H_A, past, 2 * HD_A).astype(BF16).transpose(0, 1, 3, 2)
    ctx = (ctx_kt, cache_attn_v[:, 0].astype(BF16), state_delta[:, 0])
    y_sample, _, _, _ = _trunk(x_sample, mod4, lambda i, tm: 1 + (i * tm) // ls, params, ctx)

    new_k = new_k.reshape(bp, 1, H_A, lp, 2, HD_A)
    new_v = new_v.reshape(bp, 1, H_A, lp, DV_A)
    return (y_prompt, y_sample, new_k, new_v, state[:, None])
```

```python
import functools
import math

import jax
import jax.numpy as jnp
from jax import lax
from jax.experimental import pallas as pl
from jax.experimental.pallas import tpu as pltpu

F32 = jnp.float32
BF16 = jnp.bfloat16

D_MODEL = 2048
GRID_W = 64
H_A = 8
HD_A = 64
DV_A = 128
H_B = 8
DK_B = 128
DV_B = 128
CONV_K = 5
CHUNK = 64
D_FF = 4 * D_MODEL
ROPE_BASE = 10000.0
EPS = 1e-6
N_MOD = 6
LAM_INIT = 0.8 - 0.6 * math.exp(-0.3 * 0)

LANES = 128
N_MAIN = 7168
N_TAIL = 32
ATTN_COLS = 3 * H_A * DV_A
PROJ_TILE = 512
ATTN_TILES = ATTN_COLS // PROJ_TILE
Q_SCALE = HD_A ** -0.5 * math.log2(math.e)
COL_DQ, COL_DK, COL_DV, COL_GATE = 0, 8, 16, 24
VMEM_LIMIT = 56 * 1024 * 1024
TOKEN_TILE = 1024
OUTPROJ_TILE = 512
MLP_OUT_CHUNK = 512
ATTN_Q_TILE = 1024
ATTN_ROW_BLOCK = 256
ATTN_CTX_HEADS = 4
ATTN_LATENT_HEADS = 2


def _cparams(sem):
    return pltpu.CompilerParams(dimension_semantics=sem, vmem_limit_bytes=VMEM_LIMIT)


def _sigmoid(x):
    return 0.5 * jnp.tanh(0.5 * x) + 0.5


def _silu(x):
    half = 0.5 * x
    return half * jnp.tanh(half) + half


def _softplus(x):
    return jnp.maximum(x, 0.0) + jnp.log(1.0 + jnp.exp(-jnp.abs(x)))


def _dot(a, b):
    return jnp.dot(a, b, preferred_element_type=F32)


def _bmm(a, b):
    return lax.dot_general(a, b, (((2,), (1,)), ((0,), (0,))), preferred_element_type=F32)


def _bmm_nt(a, b):
    return lax.dot_general(a, b, (((2,), (2,)), ((0,), (0,))), preferred_element_type=F32)


def _mod_kernel(c_ref, w_ref, b_ref, o_ref):
    s = _silu(c_ref[...]).astype(BF16)
    o_ref[...] = _dot(s, w_ref[...].astype(BF16)) + b_ref[...]


def _modulation(cc, w_mod, b_mod):
    rows = cc.shape[0]
    n = w_mod.shape[1]
    tn = 512
    return pl.pallas_call(
        _mod_kernel,
        grid=(n // tn,),
        in_specs=[pl.BlockSpec((rows, D_MODEL), lambda j: (0, 0)),
                  pl.BlockSpec((D_MODEL, tn), lambda j: (0, j)),
                  pl.BlockSpec((1, tn), lambda j: (0, j))],
        out_specs=pl.BlockSpec((rows, tn), lambda j: (0, j)),
        out_shape=jax.ShapeDtypeStruct((rows, n), F32),
        compiler_params=_cparams(("arbitrary",)),
        name="modulation",
    )(cc, w_mod, b_mod)


def _rope(x, cos, sin_lo, sin_hi):
    return (x * cos + pltpu.roll(x, LANES - 16, 1) * sin_lo + pltpu.roll(x, 16, 1) * sin_hi)


def _inproj_norm(x_ref, sh_ref, sc_ref, n1_ref, wt_ref, ot_ref, xn_ref):
    x = x_ref[...]
    ms = jnp.mean(x * x, axis=-1, keepdims=True)
    y = x * lax.rsqrt(ms + EPS) * n1_ref[...]
    y = y * (1.0 + sc_ref[...]) + sh_ref[...]
    xn = y.astype(BF16)
    xn_ref[...] = xn
    ot_ref[...] = _dot(xn, wt_ref[...])


def _inproj_latent_kernel(x_ref, sh_ref, sc_ref, n1_ref, w_ref, wt_ref, cos_ref, slo_ref, shi_ref,
                          q_ref, kt_ref, v_ref, main_ref, ot_ref, xn_ref):
    j = pl.program_id(1)
    q_tiles = ATTN_TILES // 3
    heads = [slice(g * LANES, (g + 1) * LANES) for g in range(PROJ_TILE // LANES)]

    @pl.when(j == 0)
    def _():
        _inproj_norm(x_ref, sh_ref, sc_ref, n1_ref, wt_ref, ot_ref, xn_ref)

    r = _dot(xn_ref[...], w_ref[...])

    @pl.when(j < q_tiles)
    def _():
        cos, slo, shi = cos_ref[...], slo_ref[...], shi_ref[...]
        for sl in heads:
            q_ref[:, sl] = (_rope(r[:, sl], cos, slo, shi) * Q_SCALE).astype(BF16)

    @pl.when((j >= q_tiles) & (j < 2 * q_tiles))
    def _():
        cos, slo, shi = cos_ref[...], slo_ref[...], shi_ref[...]
        for g, sl in enumerate(heads):
            kt_ref[g] = _rope(r[:, sl], cos, slo, shi).T.astype(BF16)

    @pl.when((j >= 2 * q_tiles) & (j < ATTN_TILES))
    def _():
        v_ref[...] = r.astype(BF16)

    @pl.when(j >= ATTN_TILES)
    def _():
        main_ref[...] = r


def _inproj_context_kernel(x_ref, sh_ref, sc_ref, n1_ref, w_ref, wt_ref,
                           q_ref, k_ref, v_ref, main_ref, ot_ref, xn_ref, *, seq):
    j = pl.program_id(1)
    q_tiles = ATTN_TILES // 3

    @pl.when(j == 0)
    def _():
        _inproj_norm(x_ref, sh_ref, sc_ref, n1_ref, wt_ref, ot_ref, xn_ref)

    r = _dot(xn_ref[...], w_ref[...])

    def store_heads(dst_ref):
        for g in range(PROJ_TILE // LANES):
            dst_ref[:, g] = r[:, g * LANES:(g + 1) * LANES].reshape(-1, seq, LANES)

    @pl.when(j < q_tiles)
    def _():
        q_ref[...] = (r * Q_SCALE).astype(BF16)

    @pl.when((j >= q_tiles) & (j < 2 * q_tiles))
    def _():
        store_heads(k_ref)

    @pl.when((j >= 2 * q_tiles) & (j < ATTN_TILES))
    def _():
        store_heads(v_ref)

    @pl.when(j >= ATTN_TILES)
    def _():
        main_ref[...] = r


def _in_projection(x2d, mod4, mod_row, norm1, w_main, w_tail, rope_tabs, seq):
    m = x2d.shape[0]
    tm, tn = TOKEN_TILE, PROJ_TILE
    n_delta = N_MAIN - ATTN_COLS
    in_specs = [
        pl.BlockSpec((tm, D_MODEL), lambda i, j: (i, 0)),
        pl.BlockSpec((None, None, 1, D_MODEL), lambda i, j: (mod_row(i, tm), 0, 0, 0)),
        pl.BlockSpec((None, None, 1, D_MODEL), lambda i, j: (mod_row(i, tm), 1, 0, 0)),
        pl.BlockSpec((1, D_MODEL), lambda i, j: (0, 0)),
        pl.BlockSpec((D_MODEL, tn), lambda i, j: (0, j)),
        pl.BlockSpec((D_MODEL, LANES), lambda i, j: (0, 0)),
    ]
    args = [x2d, mod4, mod4, norm1, w_main, w_tail]
    main_spec = pl.BlockSpec((tm, tn), lambda i, j: (i, jnp.maximum(j - ATTN_TILES, 0)))
    main_shape = jax.ShapeDtypeStruct((m, n_delta), F32)
    tail_spec = pl.BlockSpec((tm, LANES), lambda i, j: (i, 0))
    tail_shape = jax.ShapeDtypeStruct((m, LANES), F32)
    q_tiles = ATTN_TILES // 3
    heads_per_tile = tn // LANES

    def cols_spec(first_tile):
        return pl.BlockSpec((tm, tn), lambda i, j: (i, jnp.clip(j - first_tile, 0, q_tiles - 1)))

    cols_shape = jax.ShapeDtypeStruct((m, ATTN_COLS // 3), BF16)
    if rope_tabs is not None:
        assert tm == seq
        for t in rope_tabs:
            in_specs.append(pl.BlockSpec((tm, LANES), lambda i, j: (0, 0)))
            args.append(t)
        body = _inproj_latent_kernel
        kt_spec = pl.BlockSpec((None, heads_per_tile, LANES, seq),
                               lambda i, j: (i, jnp.clip(j - q_tiles, 0, q_tiles - 1), 0, 0))
        out_specs = [cols_spec(0), kt_spec, cols_spec(2 * q_tiles), main_spec, tail_spec]
        out_shape = [cols_shape, jax.ShapeDtypeStruct((m // seq, H_A, LANES, seq), BF16), cols_shape,
                     main_shape, tail_shape]
    else:
        bt = tm // seq

        def head_spec(first_tile):
            return pl.BlockSpec((bt, heads_per_tile, seq, LANES),
                                lambda i, j: (i, jnp.clip(j - first_tile, 0, q_tiles - 1), 0, 0))

        head_shape = jax.ShapeDtypeStruct((m // seq, H_A, seq, LANES), F32)
        body = functools.partial(_inproj_context_kernel, seq=seq)
        out_specs = [cols_spec(0), head_spec(q_tiles), head_spec(2 * q_tiles), main_spec, tail_spec]
        out_shape = [cols_shape, head_shape, head_shape, main_shape, tail_shape]
    return pl.pallas_call(
        body,
        grid=(m // tm, N_MAIN // tn),
        in_specs=in_specs,
        out_specs=out_specs,
        out_shape=out_shape,
        scratch_shapes=[pltpu.VMEM((tm, D_MODEL), BF16)],
        compiler_params=_cparams(("parallel", "arbitrary")),
        name="in_projection",
    )(*args)


def _rope_tables(seq_len):
    lane = jnp.arange(LANES)
    r = lane % HD_A
    axis = r // (HD_A // 2)
    u = r % (HD_A // 2)
    nf = HD_A // 4
    f = u % nf
    hi_half = u // nf
    inv = 1.0 / (ROPE_BASE ** (f.astype(F32) * 2.0 / (HD_A // 2)))
    t = jnp.arange(seq_len)
    pos = jnp.where(axis[None, :] == 0, (t // GRID_W)[:, None], (t % GRID_W)[:, None]).astype(F32)
    ang = pos * inv[None, :]
    cos, sin = jnp.cos(ang), jnp.sin(ang)
    sin_lo = jnp.where(hi_half[None, :] == 0, -sin, 0.0)
    sin_hi = jnp.where(hi_half[None, :] == 1, sin, 0.0)
    return cos, sin_lo, sin_hi


def _attn_kernel(*refs, has_ctx):
    if has_ctx:
        q_ref, k_ref, v_ref, ck_ref, cv_ref, lam_ref, g_ref, o_ref = refs
    else:
        q_ref, k_ref, v_ref, lam_ref, g_ref, o_ref = refs
    lp = lam_ref[...]
    lam = (jnp.exp(jnp.sum(lp[0:1] * lp[1:2], axis=-1, keepdims=True))
           - jnp.exp(jnp.sum(lp[2:3] * lp[3:4], axis=-1, keepdims=True)) + LAM_INIT)
    first = (lax.broadcasted_iota(jnp.int32, (1, LANES), 1) < HD_A).astype(F32)
    keep = (first.astype(BF16), (1.0 - first).astype(BF16))
    n_heads = k_ref.shape[0]
    blocks = [(slice(r, r + ATTN_ROW_BLOCK), hh) for hh in range(n_heads)
              for r in range(0, q_ref.shape[0], ATTN_ROW_BLOCK)]

    def keys_values(hh):
        if has_ctx:
            return ([ck_ref[hh], k_ref[hh]], [cv_ref[hh], v_ref[:, hh * LANES:(hh + 1) * LANES]])
        return [k_ref[hh].T.astype(BF16)], [v_ref[hh].astype(BF16)]

    def scores(block):
        rws, hh = block
        q = q_ref[rws, hh * LANES:(hh + 1) * LANES]
        keys_t, _ = keys_values(hh)
        return [[_dot(q * km, kt) for kt in keys_t] for km in keep]

    def softmax_diff(sb):
        probs = []
        for s in sb:
            mx = functools.reduce(jnp.maximum, [jnp.max(x, axis=-1, keepdims=True) for x in s])
            e = [jnp.exp2(x - mx) for x in s]
            z = functools.reduce(jnp.add, [jnp.sum(x, axis=-1, keepdims=True) for x in e])
            probs.append((e, 1.0 / z))
        (e0, r0), (e1, r1) = probs
        r1 = r1 * lam
        return [(a * r0 - b * r1).astype(BF16) for a, b in zip(e0, e1)]

    def output(block, pd):
        rws, hh = block
        _, vals = keys_values(hh)
        o = functools.reduce(jnp.add, [_dot(p, vv) for p, vv in zip(pd, vals)])
        ms = jnp.mean(o * o, axis=-1, keepdims=True)
        o = o * lax.rsqrt(ms + EPS) * g_ref[...] * (1.0 - LAM_INIT)
        o_ref[rws, hh * LANES:(hh + 1) * LANES] = o.astype(o_ref.dtype)

    sc, pd = {}, {}
    for i in range(len(blocks) + 2):
        if i < len(blocks):
            sc[i] = scores(blocks[i])
        if 0 <= i - 1 < len(blocks):
            pd[i - 1] = softmax_diff(sc.pop(i - 1))
        if 0 <= i - 2 < len(blocks):
            output(blocks[i - 2], pd.pop(i - 2))


def _diff_attention(q3, k_src, v_src, ctx_kt, ctx_v, attn_lambda, subln, tq):
    bsz, seq, _ = q3.shape
    has_ctx = ctx_kt is not None
    hps = ATTN_LATENT_HEADS if has_ctx else ATTN_CTX_HEADS
    if has_ctx:
        k_spec = pl.BlockSpec((None, hps, LANES, seq), lambda b, h, i: (b, h, 0, 0))
        v_spec = pl.BlockSpec((None, seq, hps * LANES), lambda b, h, i: (b, 0, h))
    else:
        k_spec = v_spec = pl.BlockSpec((None, hps, seq, LANES), lambda b, h, i: (b, h, 0, 0))
    in_specs = [pl.BlockSpec((None, tq, hps * LANES), lambda b, h, i: (b, i, h)), k_spec, v_spec]
    args = [q3, k_src, v_src]
    if has_ctx:
        past = ctx_v.shape[2]
        in_specs += [pl.BlockSpec((None, hps, LANES, past), lambda b, h, i: (b, h, 0, 0)),
                     pl.BlockSpec((None, hps, past, LANES), lambda b, h, i: (b, h, 0, 0))]
        args += [ctx_kt, ctx_v]
    in_specs += [pl.BlockSpec((4, HD_A), lambda b, h, i: (0, 0)),
                 pl.BlockSpec((1, DV_A), lambda b, h, i: (0, 0))]
    args += [attn_lambda, subln]
    return pl.pallas_call(
        functools.partial(_attn_kernel, has_ctx=has_ctx),
        grid=(bsz, H_A // hps, seq // tq),
        in_specs=in_specs,
        out_specs=pl.BlockSpec((None, tq, hps * LANES), lambda b, h, i: (b, i, h)),
        out_shape=jax.ShapeDtypeStruct((bsz, seq, H_A * DV_A), BF16),
        compiler_params=_cparams(("parallel", "parallel", "arbitrary")),
        name="diff_attention",
    )(*args)


GATE_G, GATE_BETA, GATE_TOT = 0, 2 * H_B, 4 * H_B
NEG_BIG = -1e30


def _gates_kernel(ab_ref, alog_ref, dtb_ref, o_ref, *, seq):
    n_chunks = seq // CHUNK
    ab = ab_ref[...]
    lane = lax.broadcasted_iota(jnp.int32, ab.shape, 1)
    pos = lax.broadcasted_iota(jnp.int32, ab.shape, 0) % CHUNK
    g = -jnp.exp(alog_ref[...]) * _softplus(ab + dtb_ref[...])
    g = jnp.where(lane < 2 * H_B, g, 0.0)
    fwd = g
    bwd = g
    sft = 1
    while sft < CHUNK:
        fwd = fwd + jnp.where(pos >= sft, pltpu.roll(fwd, sft, 0), 0.0)
        bwd = bwd + jnp.where(pos < CHUNK - sft, pltpu.roll(bwd, seq - sft, 0), 0.0)
        sft *= 2
    cs = jnp.where(lane < H_B, fwd, bwd)
    tot = jnp.sum(g.reshape(n_chunks, CHUNK, LANES), axis=1, keepdims=True)
    tot = jnp.broadcast_to(tot, (n_chunks, CHUNK, LANES)).reshape(seq, LANES)
    tot = pltpu.roll(tot, GATE_TOT, 1)
    out = jnp.where(lane < GATE_BETA, cs, jnp.where(lane < GATE_TOT, _sigmoid(ab), tot))
    o_ref[...] = out


def _delta_gates(tail3, alog_row, dtb_row):
    bsz, seq, _ = tail3.shape
    return pl.pallas_call(
        functools.partial(_gates_kernel, seq=seq),
        grid=(bsz,),
        in_specs=[pl.BlockSpec((None, seq, LANES), lambda b: (b, 0, 0)),
                  pl.BlockSpec((1, LANES), lambda b: (0, 0)),
                  pl.BlockSpec((1, LANES), lambda b: (0, 0))],
        out_specs=pl.BlockSpec((None, seq, LANES), lambda b: (b, 0, 0)),
        out_shape=jax.ShapeDtypeStruct((bsz, seq, LANES), F32),
        compiler_params=_cparams(("parallel",)),
        name="delta_gates",
    )(tail3, alog_row, dtb_row)


PAD_ROWS = 8


def _conv_silu(x_ref, cols, w, pad_ref):
    seq = x_ref.shape[0]
    pad_ref[PAD_ROWS:PAD_ROWS + seq, :] = x_ref[:, cols]
    y = None
    for t in range(CONV_K):
        s = PAD_ROWS + t - CONV_K // 2
        term = pad_ref[s:s + seq, :] * w[t:t + 1]
        y = term if y is None else y + term
    return _silu(y)


def _lane_col(x, col):
    lane = lax.broadcasted_iota(jnp.int32, x.shape, 1)
    c = jnp.sum(jnp.where(lane == col, x, 0.0), axis=-1, keepdims=True)
    return jnp.broadcast_to(c, x.shape)


def _delta_kernel(*refs, seq, hps, group, has_s0, emit_state):
    it = iter(refs)
    qr_ref, kr_ref, vr_ref, gate_ref, tok_ref = (next(it) for _ in range(5))
    wq_ref, wk_ref, wv_ref, nw_ref = (next(it) for _ in range(4))
    s0_ref = next(it) if has_s0 else None
    o_ref = next(it)
    st_ref = next(it) if emit_state else None
    kq_s, rhs_s, qg_s, akg_s, g_s, beta_s, egl_s, lhs_s, add_s, oacc_s, s_s, pad_s = it

    hg = pl.program_id(1)
    n_chunks = seq // CHUNK
    n_inst = hps * n_chunks
    left = lax.broadcasted_iota(jnp.int32, (seq, LANES), 1) < CHUNK
    tok = tok_ref[...]
    pad_s[0:PAD_ROWS, :] = jnp.zeros((PAD_ROWS, LANES), F32)
    pad_s[PAD_ROWS + seq:2 * PAD_ROWS + seq, :] = jnp.zeros((PAD_ROWS, LANES), F32)

    def phase_a(hh):
        cols = slice(hh * LANES, (hh + 1) * LANES)
        head = hg * hps + hh
        inst = slice(hh * n_chunks, (hh + 1) * n_chunks)
        rows = slice(hh * seq, (hh + 1) * seq)
        q = _conv_silu(qr_ref, cols, wq_ref[:, cols], pad_s)
        k = _conv_silu(kr_ref, cols, wk_ref[:, cols], pad_s)
        v = _conv_silu(vr_ref, cols, wv_ref[:, cols], pad_s)
        q = q * lax.rsqrt(jnp.sum(q * q, axis=-1, keepdims=True) + EPS) * (DK_B ** -0.5)
        k = k * lax.rsqrt(jnp.sum(k * k, axis=-1, keepdims=True) + EPS)
        kq_s[inst, 0:CHUNK, :] = k.astype(BF16).reshape(n_chunks, CHUNK, LANES)
        kq_s[inst, CHUNK:2 * CHUNK, :] = q.astype(BF16).reshape(n_chunks, CHUNK, LANES)
        packed = []
        for d in range(2):
            cs = _lane_col(tok, GATE_G + d * H_B + head)
            beta = _lane_col(tok, GATE_BETA + d * H_B + head)
            tot = _lane_col(tok, GATE_TOT + d * H_B + head)
            eg = jnp.exp(cs)
            drows = slice(d * CHUNK, (d + 1) * CHUNK)
            rhs_s[inst, drows, 0:LANES] = (v * beta).astype(BF16).reshape(n_chunks, CHUNK, LANES)
            rhs_s[inst, drows, LANES:2 * LANES] = (k * (beta * eg)).astype(BF16).reshape(n_chunks, CHUNK, LANES)
            qg_s[d, inst] = (q * eg).reshape(n_chunks, CHUNK, LANES)
            kg = (k * jnp.exp(tot - cs)).reshape(n_chunks, CHUNK, LANES)
            packed.append((cs, beta, kg))
            egl_s[d, inst] = jnp.exp(tot.reshape(n_chunks, CHUNK, LANES)[:, 0:8, :])
        g_s[rows, :] = jnp.where(left, packed[0][0], packed[1][0])
        beta_s[rows, :] = jnp.where(left, packed[0][1], packed[1][1])
        for c in range(n_chunks):
            kg_pair = jnp.concatenate([packed[0][2][c], packed[1][2][c]], axis=0)
            akg_s[hh * n_chunks + c, CHUNK:CHUNK + DK_B, :] = kg_pair.T.astype(BF16)

    ri = lax.broadcasted_iota(jnp.int32, (CHUNK, LANES), 0)
    cj = lax.broadcasted_iota(jnp.int32, (CHUNK, LANES), 1)
    left2 = cj < CHUNK
    cj = jnp.where(left2, cj, cj - CHUNK)
    diag2 = (ri == cj)[None]
    eye2 = (ri == cj).astype(F32)[None]
    incl2 = (jnp.where(left2, ri - cj, cj - ri) >= 0)[None]
    keep_l = left2[0:1].astype(F32).astype(BF16)[None]
    keep_r = (1.0 - left2[0:1].astype(F32)).astype(BF16)[None]

    def blockdiag(x):
        return jnp.concatenate([x * keep_l, x * keep_r], axis=1)

    def phase_b(i0):
        inst = slice(i0, i0 + group)
        rows = slice(i0 * CHUNK, (i0 + group) * CHUNK)
        kq = kq_s[inst]
        kc = kq[:, 0:CHUNK, :]
        kkqk = _bmm_nt(kq, jnp.concatenate([kc, kc], axis=1))
        kk, qk = kkqk[:, 0:CHUNK, :], kkqk[:, CHUNK:2 * CHUNK, :]
        gcol = g_s[rows, :].reshape(group, CHUNK, LANES)
        bcol = beta_s[rows, :].reshape(group, CHUNK, LANES)
        grow = jnp.sum(eye2 * gcol, axis=1, keepdims=True)
        decay = jnp.exp(jnp.where(incl2, gcol - grow, NEG_BIG))
        akg_s[inst, 0:CHUNK, :] = (qk * decay).astype(BF16)
        nm = jnp.where(diag2, 0.0, -(bcol * kk * decay))
        t = eye2 + nm
        nb = nm.astype(BF16)
        p = _bmm(nb, blockdiag(nb))
        for _ in range(4):
            pb = p.astype(BF16)
            tp = _bmm(jnp.concatenate([t.astype(BF16), pb], axis=1), blockdiag(pb))
            t = t + tp[:, 0:CHUNK, :]
            p = tp[:, CHUNK:2 * CHUNK, :]
        t = t + _bmm(t.astype(BF16), blockdiag(p.astype(BF16)))
        tb = t.astype(BF16)
        resid = (eye2 - t) + _bmm(nb, blockdiag(tb))
        t = t + _bmm(tb, blockdiag(resid.astype(BF16)))
        uw = _bmm(blockdiag(t.astype(BF16)), rhs_s[inst])
        uwb = uw.astype(BF16)
        akg = akg_s[inst]
        for d, keep in enumerate((keep_l, keep_r)):
            res = _bmm(akg * keep, uwb)
            lhs_s[d, inst, 0:CHUNK, :] = (qg_s[d, inst] - res[:, 0:CHUNK, LANES:2 * LANES]).astype(BF16)
            lhs_s[d, inst, CHUNK:CHUNK + DK_B, :] = (-res[:, CHUNK:CHUNK + DK_B, LANES:2 * LANES]).astype(BF16)
            add_s[d, inst] = res[:, :, 0:LANES]

    heads_per_group = max(1, group // n_chunks)
    for i0 in range(0, n_inst, group):
        first = i0 // n_chunks
        for hh in range(first, min(hps, first + heads_per_group)):
            phase_a(hh)
        phase_b(i0)

    for hh in range(hps):
        for d in range(2):
            s_s[d, hh] = s0_ref[d, hh] if has_s0 else jnp.zeros((DK_B, DV_B), F32)

    def scan_step(t, carry):
        for hh in range(hps):
            for d in range(2):
                c = hh * n_chunks + (t if d == 0 else n_chunks - 1 - t)
                rows = pl.ds(pl.multiple_of(c * CHUNK, CHUNK), CHUNK)
                s = s_s[d, hh]
                r = _dot(lhs_s[d, c], s.astype(BF16)) + add_s[d, c]
                oacc_s[d, rows, :] = r[0:CHUNK]
                s_s[d, hh] = egl_s[d, c][0:1, :] * s + r[CHUNK:CHUNK + DK_B]
        return carry

    lax.fori_loop(0, n_chunks, scan_step, 0)

    for hh in range(hps):
        cols = slice(hh * LANES, (hh + 1) * LANES)
        rows = slice(hh * seq, (hh + 1) * seq)
        if emit_state:
            st_ref[0, hh] = s_s[0, hh]
            st_ref[1, hh] = s_s[1, hh]
        o = oacc_s[0, rows, :] + oacc_s[1, rows, :]
        ms = jnp.mean(o * o, axis=-1, keepdims=True)
        o = o * lax.rsqrt(ms + EPS) * nw_ref[...] * _silu(gate_ref[:, cols])
        o_ref[:, cols] = o.astype(o_ref.dtype)


def _delta_mixer(proj3, tok3, conv_w, norm_w, s0, emit_state):
    bsz, seq, _ = proj3.shape
    hps = max(2, min(H_B, 2048 // seq))
    has_s0 = s0 is not None
    n_chunks = seq // CHUNK
    n_inst = hps * n_chunks
    group = min(n_inst, 16)
    wide = hps * LANES

    def col(cb):
        return pl.BlockSpec((None, seq, wide), lambda b, g: (b, 0, cb // hps + g))

    def wcol(cb):
        return pl.BlockSpec((CONV_K, wide), lambda b, g: (0, cb // hps + g))

    in_specs = [col(COL_DQ), col(COL_DK), col(COL_DV), col(COL_GATE),
                pl.BlockSpec((None, seq, LANES), lambda b, g: (b, 0, 0)),
                wcol(0), wcol(H_B), wcol(2 * H_B),
                pl.BlockSpec((1, DV_B), lambda b, g: (0, 0))]
    args = [proj3, proj3, proj3, proj3, tok3, conv_w, conv_w, conv_w, norm_w]
    state_spec = pl.BlockSpec((None, 2, hps, DK_B, DV_B), lambda b, g: (b, 0, g, 0, 0))
    if has_s0:
        in_specs.append(state_spec)
        args.append(s0)
    out_specs = [pl.BlockSpec((None, seq, wide), lambda b, g: (b, 0, g))]
    out_shape = [jax.ShapeDtypeStruct((bsz, seq, H_B * DV_B), BF16)]
    if emit_state:
        out_specs.append(state_spec)
        out_shape.append(jax.ShapeDtypeStruct((bsz, 2, H_B, DK_B, DV_B), F32))
    scratch = [
        pltpu.VMEM((n_inst, 2 * CHUNK, LANES), BF16),
        pltpu.VMEM((n_inst, 2 * CHUNK, 2 * LANES), BF16),
        pltpu.VMEM((2, n_inst, CHUNK, LANES), F32),
        pltpu.VMEM((n_inst, CHUNK + DK_B, LANES), BF16),
        pltpu.VMEM((hps * seq, LANES), F32),
        pltpu.VMEM((hps * seq, LANES), F32),
        pltpu.VMEM((2, n_inst, 8, LANES), F32),
        pltpu.VMEM((2, n_inst, CHUNK + DK_B, LANES), BF16),
        pltpu.VMEM((2, n_inst, CHUNK + DK_B, LANES), F32),
        pltpu.VMEM((2, hps * seq, LANES), F32),
        pltpu.VMEM((2, hps, DK_B, DV_B), F32),
        pltpu.VMEM((seq + 2 * PAD_ROWS, LANES), F32),
    ]
    outs = pl.pallas_call(
        functools.partial(_delta_kernel, seq=seq, hps=hps, group=group, has_s0=has_s0,
                          emit_state=emit_state),
        grid=(bsz, H_B // hps),
        in_specs=in_specs,
        out_specs=out_specs,
        out_shape=out_shape,
        scratch_shapes=scratch,
        compiler_params=_cparams(("parallel", "parallel")),
        name="delta_mixer",
    )(*args)
    return outs if emit_state else (outs[0], None)


def _outproj_kernel(x_ref, oa_ref, ob_ref, g1_ref, sh_ref, sc_ref, n2_ref, wa_ref, wb_ref, h_ref, hn_ref):
    mixed = _dot(oa_ref[...], wa_ref[...]) + _dot(ob_ref[...], wb_ref[...])
    h = x_ref[...] + g1_ref[...] * mixed
    h_ref[...] = h
    ms = jnp.mean(h * h, axis=-1, keepdims=True)
    y = h * lax.rsqrt(ms + EPS) * n2_ref[...]
    hn_ref[...] = (y * (1.0 + sc_ref[...]) + sh_ref[...]).astype(BF16)


def _out_projection(x2d, oa2d, ob2d, mod4, mod_row, norm2, w_out):
    m = x2d.shape[0]
    tm = OUTPROJ_TILE
    half = w_out.shape[0] // 2

    def mod_spec(k):
        return pl.BlockSpec((None, None, 1, D_MODEL), lambda i: (mod_row(i, tm), k, 0, 0))

    def weight_spec(part):
        return pl.BlockSpec((half, D_MODEL), lambda i: (part, 0), pipeline_mode=pl.Buffered(1))

    return pl.pallas_call(
        _outproj_kernel,
        grid=(m // tm,),
        in_specs=[pl.BlockSpec((tm, D_MODEL), lambda i: (i, 0)),
                  pl.BlockSpec((tm, half), lambda i: (i, 0)),
                  pl.BlockSpec((tm, half), lambda i: (i, 0)),
                  mod_spec(2), mod_spec(3), mod_spec(4),
                  pl.BlockSpec((1, D_MODEL), lambda i: (0, 0)),
                  weight_spec(0), weight_spec(1)],
        out_specs=[pl.BlockSpec((tm, D_MODEL), lambda i: (i, 0)),
                   pl.BlockSpec((tm, D_MODEL), lambda i: (i, 0))],
        out_shape=[jax.ShapeDtypeStruct((m, D_MODEL), F32),
                   jax.ShapeDtypeStruct((m, D_MODEL), BF16)],
        compiler_params=_cparams(("parallel",)),
        name="out_projection",
    )(x2d, oa2d, ob2d, mod4, mod4, mod4, norm2, w_out, w_out)


def _mlp_kernel(hn_ref, h_ref, g2_ref, nf_ref, w1_ref, w2_ref, y_ref):
    f = pl.program_id(1)

    @pl.when(f == 0)
    def _():
        y_ref[...] = jnp.zeros_like(y_ref)

    a = jnp.maximum(_dot(hn_ref[...], w1_ref[...]), 0.0)
    a2 = (a * a).astype(BF16)
    for n0 in range(0, D_MODEL, MLP_OUT_CHUNK):
        sl = slice(n0, n0 + MLP_OUT_CHUNK)
        y_ref[:, sl] += _dot(a2, w2_ref[:, sl])

    @pl.when(f == pl.num_programs(1) - 1)
    def _():
        y = h_ref[...] + g2_ref[...] * y_ref[...]
        ms = jnp.mean(y * y, axis=-1, keepdims=True)
        y_ref[...] = y * lax.rsqrt(ms + EPS) * nf_ref[...]


def _mlp(hn2d, h2d, mod4, mod_row, norm_f, w1, w2, tf=512):
    m = h2d.shape[0]
    tm = TOKEN_TILE
    row_tile = pl.BlockSpec((tm, D_MODEL), lambda i, f: (i, 0), pipeline_mode=pl.Buffered(1))
    return pl.pallas_call(
        _mlp_kernel,
        grid=(m // tm, D_FF // tf),
        in_specs=[row_tile, row_tile,
                  pl.BlockSpec((None, None, 1, D_MODEL), lambda i, f: (mod_row(i, tm), 5, 0, 0)),
                  pl.BlockSpec((1, D_MODEL), lambda i, f: (0, 0)),
                  pl.BlockSpec((D_MODEL, tf), lambda i, f: (0, f)),
                  pl.BlockSpec((tf, D_MODEL), lambda i, f: (f, 0))],
        out_specs=pl.BlockSpec((tm, D_MODEL), lambda i, f: (i, 0)),
        out_shape=jax.ShapeDtypeStruct((m, D_MODEL), F32),
        compiler_params=_cparams(("parallel", "arbitrary")),
        name="mlp",
    )(hn2d, h2d, mod4, norm_f, w1, w2)


def _trunk(x, mod4, mod_row, params, ctx):
    (w_main, w_tail, conv_w, alog_row, dtb_row, delta_norm, attn_lambda, attn_subln, norm1,
     w_out, norm2, w_ff1, w_ff2, norm_f) = params
    bsz, seq, _ = x.shape
    x2d = x.reshape(bsz * seq, D_MODEL)
    if ctx is None:
        q, new_k, new_v, main, tail = _in_projection(x2d, mod4, mod_row, norm1, w_main, w_tail, None, seq)
        oa = _diff_attention(q.reshape(bsz, seq, -1), new_k, new_v, None, None, attn_lambda, attn_subln,
                             tq=seq)
    else:
        ctx_kt, ctx_v, s0 = ctx
        new_k = new_v = None
        q, k_t, v, main, tail = _in_projection(x2d, mod4, mod_row, norm1, w_main, w_tail,
                                               _rope_tables(seq), seq)
        oa = _diff_attention(q.reshape(bsz, seq, -1), k_t, v.reshape(bsz, seq, -1), ctx_kt, ctx_v,
                             attn_lambda, attn_subln, tq=ATTN_Q_TILE)
    main3 = main.reshape(bsz, seq, N_MAIN - ATTN_COLS)
    tok3 = _delta_gates(tail.reshape(bsz, seq, LANES), alog_row, dtb_row)
    ob, state = _delta_mixer(main3, tok3, conv_w, delta_norm, None if ctx is None else s0, ctx is None)
    h, hn = _out_projection(x2d, oa.reshape(bsz * seq, -1), ob.reshape(bsz * seq, -1), mod4, mod_row,
                            norm2, w_out)
    y = _mlp(hn, h, mod4, mod_row, norm_f, w_ff1, w_ff2)
    return y.reshape(bsz, seq, D_MODEL), new_k, new_v, state


def kernel(x_prompt, x_sample, cache_attn_k, cache_attn_v, state_delta, c, c_ctx, w_mod, b_mod, norm1, w_in, conv_w, a_log, dt_bias, delta_norm, attn_lambda, attn_subln, w_out, norm2, w_ff1, w_ff2, norm_f):
    bp, lp, _ = x_prompt.shape
    bs, ls, _ = x_sample.shape
    past = cache_attn_k.shape[3]

    rows = 16
    cc = jnp.concatenate([c_ctx[None, :], c, jnp.zeros((rows - 1 - bs, D_MODEL), F32)], axis=0)
    mod = _modulation(cc, w_mod[0], b_mod[0][None, :])
    mod4 = mod.reshape(rows, N_MOD, 1, D_MODEL)

    w_main = w_in[0].astype(BF16)
    w_tail = jnp.pad(w_in[0][:, N_MAIN:], ((0, 0), (0, LANES - N_TAIL))).astype(BF16)
    alog_row = jnp.pad(a_log[0].reshape(1, 2 * H_B), ((0, 0), (0, LANES - 2 * H_B)))
    dtb_row = jnp.pad(dt_bias[0].reshape(1, 2 * H_B), ((0, 0), (0, LANES - 2 * H_B)))
    params = (w_main, w_tail, conv_w[0], alog_row, dtb_row, delta_norm[0][None, :], attn_lambda[0],
              attn_subln[0][None, :], norm1[0][None, :], w_out[0].astype(BF16), norm2[0][None, :],
              w_ff1[0].astype(BF16), w_ff2[0].astype(BF16), norm_f[None, :])

    y_prompt, new_k, new_v, state = _trunk(x_prompt, mod4, lambda i, tm: 0, params, None)
    ctx_kt = cache_attn_k[:, 0].reshape(bs, H_A, past, 2 * HD_A).astype(BF16).transpose(0, 1, 3, 2)
    ctx = (ctx_kt, cache_attn_v[:, 0].astype(BF16), state_delta[:, 0])
    y_sample, _, _, _ = _trunk(x_sample, mod4, lambda i, tm: 1 + (i * tm) // ls, params, ctx)

    new_k = new_k.reshape(bp, 1, H_A, lp, 2, HD_A)
    new_v = new_v.reshape(bp, 1, H_A, lp, DV_A)
    return (y_prompt, y_sample, new_k, new_v, state[:, None])
```

```python
import functools
import math

import jax
import jax.numpy as jnp
from jax import lax
from jax.experimental import pallas as pl
from jax.experimental.pallas import tpu as pltpu

F32 = jnp.float32
BF16 = jnp.bfloat16

D_MODEL = 2048
GRID_W = 64
H_A = 8
HD_A = 64
DV_A = 128
H_B = 8
DK_B = 128
DV_B = 128
CONV_K = 5
CHUNK = 64
D_FF = 4 * D_MODEL
ROPE_BASE = 10000.0
EPS = 1e-6
N_MOD = 6
LAM_INIT = 0.8 - 0.6 * math.exp(-0.3 * 0)

LANES = 128
N_MAIN = 7168
N_TAIL = 32
ATTN_COLS = 3 * H_A * DV_A
PROJ_TILE = 512
ATTN_TILES = ATTN_COLS // PROJ_TILE
Q_SCALE = HD_A ** -0.5 * math.log2(math.e)
COL_DQ, COL_DK, COL_DV, COL_GATE = 0, 8, 16, 24
VMEM_LIMIT = 56 * 1024 * 1024
TOKEN_TILE = 1024
OUTPROJ_TILE = 512
MLP_OUT_CHUNK = 512
ATTN_Q_TILE = 1024
ATTN_ROW_BLOCK = 256
ATTN_CTX_HEADS = 4
ATTN_LATENT_HEADS = 2


def _cparams(sem):
    return pltpu.CompilerParams(dimension_semantics=sem, vmem_limit_bytes=VMEM_LIMIT)


def _sigmoid(x):
    return 0.5 * jnp.tanh(0.5 * x) + 0.5


def _silu(x):
    half = 0.5 * x
    return half * jnp.tanh(half) + half


def _softplus(x):
    return jnp.maximum(x, 0.0) + jnp.log(1.0 + jnp.exp(-jnp.abs(x)))


def _dot(a, b):
    return jnp.dot(a, b, preferred_element_type=F32)


def _bmm(a, b):
    return lax.dot_general(a, b, (((2,), (1,)), ((0,), (0,))), preferred_element_type=F32)


def _bmm_nt(a, b):
    return lax.dot_general(a, b, (((2,), (2,)), ((0,), (0,))), preferred_element_type=F32)


def _mod_kernel(c_ref, w_ref, b_ref, o_ref):
    s = _silu(c_ref[...]).astype(BF16)
    o_ref[...] = _dot(s, w_ref[...].astype(BF16)) + b_ref[...]


def _modulation(cc, w_mod, b_mod):
    rows = cc.shape[0]
    n = w_mod.shape[1]
    tn = 512
    return pl.pallas_call(
        _mod_kernel,
        grid=(n // tn,),
        in_specs=[pl.BlockSpec((rows, D_MODEL), lambda j: (0, 0)),
                  pl.BlockSpec((D_MODEL, tn), lambda j: (0, j)),
                  pl.BlockSpec((1, tn), lambda j: (0, j))],
        out_specs=pl.BlockSpec((rows, tn), lambda j: (0, j)),
        out_shape=jax.ShapeDtypeStruct((rows, n), F32),
        compiler_params=_cparams(("arbitrary",)),
        name="modulation",
    )(cc, w_mod, b_mod)


def _rope(x, cos, sin_lo, sin_hi):
    return (x * cos + pltpu.roll(x, LANES - 16, 1) * sin_lo + pltpu.roll(x, 16, 1) * sin_hi)


def _inproj_norm(x_ref, sh_ref, sc_ref, n1_ref, wt_ref, ot_ref, xn_ref):
    x = x_ref[...]
    ms = jnp.mean(x * x, axis=-1, keepdims=True)
    y = x * lax.rsqrt(ms + EPS) * n1_ref[...]
    y = y * (1.0 + sc_ref[...]) + sh_ref[...]
    xn = y.astype(BF16)
    xn_ref[...] = xn
    ot_ref[...] = _dot(xn, wt_ref[...])


def _inproj_latent_kernel(x_ref, sh_ref, sc_ref, n1_ref, w_ref, wt_ref, cos_ref, slo_ref, shi_ref,
                          q_ref, kt_ref, v_ref, main_ref, ot_ref, xn_ref):
    j = pl.program_id(1)
    q_tiles = ATTN_TILES // 3

    @pl.when(j == 0)
    def _():
        _inproj_norm(x_ref, sh_ref, sc_ref, n1_ref, wt_ref, ot_ref, xn_ref)

    def roped_heads():
        cos, slo, shi = cos_ref[...], slo_ref[...], shi_ref[...]
        half = PROJ_TILE // 2
        parts = [_dot(xn_ref[...], w_ref[:, c:c + half]) for c in range(0, PROJ_TILE, half)]
        for p, r in enumerate(parts):
            for g in range(half // LANES):
                yield p * (half // LANES) + g, _rope(r[:, g * LANES:(g + 1) * LANES], cos, slo, shi)

    @pl.when(j < q_tiles)
    def _():
        for g, x in roped_heads():
            q_ref[:, g * LANES:(g + 1) * LANES] = (x * Q_SCALE).astype(BF16)

    @pl.when((j >= q_tiles) & (j < 2 * q_tiles))
    def _():
        for g, x in roped_heads():
            kt_ref[g] = x.T.astype(BF16)

    @pl.when((j >= 2 * q_tiles) & (j < ATTN_TILES))
    def _():
        v_ref[...] = _dot(xn_ref[...], w_ref[...]).astype(BF16)

    @pl.when(j >= ATTN_TILES)
    def _():
        main_ref[...] = _dot(xn_ref[...], w_ref[...])


def _inproj_context_kernel(x_ref, sh_ref, sc_ref, n1_ref, w_ref, wt_ref,
                           q_ref, k_ref, v_ref, main_ref, ot_ref, xn_ref, *, seq):
    j = pl.program_id(1)
    q_tiles = ATTN_TILES // 3

    @pl.when(j == 0)
    def _():
        _inproj_norm(x_ref, sh_ref, sc_ref, n1_ref, wt_ref, ot_ref, xn_ref)

    def product():
        return _dot(xn_ref[...], w_ref[...])

    def store_heads(dst_ref):
        r = product()
        for g in range(PROJ_TILE // LANES):
            dst_ref[:, g] = r[:, g * LANES:(g + 1) * LANES].reshape(-1, seq, LANES)

    @pl.when(j < q_tiles)
    def _():
        q_ref[...] = (product() * Q_SCALE).astype(BF16)

    @pl.when((j >= q_tiles) & (j < 2 * q_tiles))
    def _():
        store_heads(k_ref)

    @pl.when((j >= 2 * q_tiles) & (j < ATTN_TILES))
    def _():
        store_heads(v_ref)

    @pl.when(j >= ATTN_TILES)
    def _():
        main_ref[...] = product()


def _in_projection(x2d, mod4, mod_row, norm1, w_main, w_tail, rope_tabs, seq):
    m = x2d.shape[0]
    tm, tn = TOKEN_TILE, PROJ_TILE
    n_delta = N_MAIN - ATTN_COLS
    in_specs = [
        pl.BlockSpec((tm, D_MODEL), lambda i, j: (i, 0)),
        pl.BlockSpec((None, None, 1, D_MODEL), lambda i, j: (mod_row(i, tm), 0, 0, 0)),
        pl.BlockSpec((None, None, 1, D_MODEL), lambda i, j: (mod_row(i, tm), 1, 0, 0)),
        pl.BlockSpec((1, D_MODEL), lambda i, j: (0, 0)),
        pl.BlockSpec((D_MODEL, tn), lambda i, j: (0, j)),
        pl.BlockSpec((D_MODEL, LANES), lambda i, j: (0, 0)),
    ]
    args = [x2d, mod4, mod4, norm1, w_main, w_tail]
    main_spec = pl.BlockSpec((tm, tn), lambda i, j: (i, jnp.maximum(j - ATTN_TILES, 0)))
    main_shape = jax.ShapeDtypeStruct((m, n_delta), F32)
    tail_spec = pl.BlockSpec((tm, LANES), lambda i, j: (i, 0))
    tail_shape = jax.ShapeDtypeStruct((m, LANES), F32)
    q_tiles = ATTN_TILES // 3
    heads_per_tile = tn // LANES

    def cols_spec(first_tile):
        return pl.BlockSpec((tm, tn), lambda i, j: (i, jnp.clip(j - first_tile, 0, q_tiles - 1)))

    cols_shape = jax.ShapeDtypeStruct((m, ATTN_COLS // 3), BF16)
    if rope_tabs is not None:
        assert tm == seq
        for t in rope_tabs:
            in_specs.append(pl.BlockSpec((tm, LANES), lambda i, j: (0, 0)))
            args.append(t)
        body = _inproj_latent_kernel
        kt_spec = pl.BlockSpec((None, heads_per_tile, LANES, seq),
                               lambda i, j: (i, jnp.clip(j - q_tiles, 0, q_tiles - 1), 0, 0))
        out_specs = [cols_spec(0), kt_spec, cols_spec(2 * q_tiles), main_spec, tail_spec]
        out_shape = [cols_shape, jax.ShapeDtypeStruct((m // seq, H_A, LANES, seq), BF16), cols_shape,
                     main_shape, tail_shape]
    else:
        bt = tm // seq

        def head_spec(first_tile):
            return pl.BlockSpec((bt, heads_per_tile, seq, LANES),
                                lambda i, j: (i, jnp.clip(j - first_tile, 0, q_tiles - 1), 0, 0))

        head_shape = jax.ShapeDtypeStruct((m // seq, H_A, seq, LANES), F32)
        body = functools.partial(_inproj_context_kernel, seq=seq)
        out_specs = [cols_spec(0), head_spec(q_tiles), head_spec(2 * q_tiles), main_spec, tail_spec]
        out_shape = [cols_shape, head_shape, head_shape, main_shape, tail_shape]
    return pl.pallas_call(
        body,
        grid=(m // tm, N_MAIN // tn),
        in_specs=in_specs,
        out_specs=out_specs,
        out_shape=out_shape,
        scratch_shapes=[pltpu.VMEM((tm, D_MODEL), BF16)],
        compiler_params=_cparams(("parallel", "arbitrary")),
        name="in_projection",
    )(*args)


def _rope_tables(seq_len):
    lane = jnp.arange(LANES)
    r = lane % HD_A
    axis = r // (HD_A // 2)
    u = r % (HD_A // 2)
    nf = HD_A // 4
    f = u % nf
    hi_half = u // nf
    inv = 1.0 / (ROPE_BASE ** (f.astype(F32) * 2.0 / (HD_A // 2)))
    t = jnp.arange(seq_len)
    pos = jnp.where(axis[None, :] == 0, (t // GRID_W)[:, None], (t % GRID_W)[:, None]).astype(F32)
    ang = pos * inv[None, :]
    cos, sin = jnp.cos(ang), jnp.sin(ang)
    sin_lo = jnp.where(hi_half[None, :] == 0, -sin, 0.0)
    sin_hi = jnp.where(hi_half[None, :] == 1, sin, 0.0)
    return cos, sin_lo, sin_hi


def _attn_kernel(*refs, has_ctx):
    if has_ctx:
        q_ref, k_ref, v_ref, ck_ref, cv_ref, lam_ref, g_ref, o_ref = refs
    else:
        q_ref, k_ref, v_ref, lam_ref, g_ref, o_ref = refs
    lp = lam_ref[...]
    lam = (jnp.exp(jnp.sum(lp[0:1] * lp[1:2], axis=-1, keepdims=True))
           - jnp.exp(jnp.sum(lp[2:3] * lp[3:4], axis=-1, keepdims=True)) + LAM_INIT)
    first = (lax.broadcasted_iota(jnp.int32, (1, LANES), 1) < HD_A).astype(F32)
    keep = (first.astype(BF16), (1.0 - first).astype(BF16))
    n_heads = k_ref.shape[0]
    blocks = [(slice(r, r + ATTN_ROW_BLOCK), hh) for hh in range(n_heads)
              for r in range(0, q_ref.shape[0], ATTN_ROW_BLOCK)]

    def keys_values(hh):
        if has_ctx:
            return ([ck_ref[hh], k_ref[hh]], [cv_ref[hh], v_ref[:, hh * LANES:(hh + 1) * LANES]])
        return [k_ref[hh].T.astype(BF16)], [v_ref[hh].astype(BF16)]

    def scores(block):
        rws, hh = block
        q = q_ref[rws, hh * LANES:(hh + 1) * LANES]
        keys_t, _ = keys_values(hh)
        return [[_dot(q * km, kt) for kt in keys_t] for km in keep]

    def softmax_diff(sb):
        probs = []
        for s in sb:
            mx = functools.reduce(jnp.maximum, [jnp.max(x, axis=-1, keepdims=True) for x in s])
            e = [jnp.exp2(x - mx) for x in s]
            z = functools.reduce(jnp.add, [jnp.sum(x, axis=-1, keepdims=True) for x in e])
            probs.append((e, 1.0 / z))
        (e0, r0), (e1, r1) = probs
        r1 = r1 * lam
        return [(a * r0 - b * r1).astype(BF16) for a, b in zip(e0, e1)]

    def output(block, pd):
        rws, hh = block
        _, vals = keys_values(hh)
        o = functools.reduce(jnp.add, [_dot(p, vv) for p, vv in zip(pd, vals)])
        ms = jnp.mean(o * o, axis=-1, keepdims=True)
        o = o * lax.rsqrt(ms + EPS) * g_ref[...] * (1.0 - LAM_INIT)
        o_ref[rws, hh * LANES:(hh + 1) * LANES] = o.astype(o_ref.dtype)

    sc, pd = {}, {}
    for i in range(len(blocks) + 2):
        if i < len(blocks):
            sc[i] = scores(blocks[i])
        if 0 <= i - 1 < len(blocks):
            pd[i - 1] = softmax_diff(sc.pop(i - 1))
        if 0 <= i - 2 < len(blocks):
            output(blocks[i - 2], pd.pop(i - 2))


def _diff_attention(q3, k_src, v_src, ctx_kt, ctx_v, attn_lambda, subln, tq):
    bsz, seq, _ = q3.shape
    has_ctx = ctx_kt is not None
    hps = ATTN_LATENT_HEADS if has_ctx else ATTN_CTX_HEADS
    if has_ctx:
        k_spec = pl.BlockSpec((None, hps, LANES, seq), lambda b, h, i: (b, h, 0, 0))
        v_spec = pl.BlockSpec((None, seq, hps * LANES), lambda b, h, i: (b, 0, h))
    else:
        k_spec = v_spec = pl.BlockSpec((None, hps, seq, LANES), lambda b, h, i: (b, h, 0, 0))
    in_specs = [pl.BlockSpec((None, tq, hps * LANES), lambda b, h, i: (b, i, h)), k_spec, v_spec]
    args = [q3, k_src, v_src]
    if has_ctx:
        past = ctx_v.shape[2]
        in_specs += [pl.BlockSpec((None, hps, LANES, past), lambda b, h, i: (b, h, 0, 0)),
                     pl.BlockSpec((None, hps, past, LANES), lambda b, h, i: (b, h, 0, 0))]
        args += [ctx_kt, ctx_v]
    in_specs += [pl.BlockSpec((4, HD_A), lambda b, h, i: (0, 0)),
                 pl.BlockSpec((1, DV_A), lambda b, h, i: (0, 0))]
    args += [attn_lambda, subln]
    return pl.pallas_call(
        functools.partial(_attn_kernel, has_ctx=has_ctx),
        grid=(bsz, H_A // hps, seq // tq),
        in_specs=in_specs,
        out_specs=pl.BlockSpec((None, tq, hps * LANES), lambda b, h, i: (b, i, h)),
        out_shape=jax.ShapeDtypeStruct((bsz, seq, H_A * DV_A), BF16),
        compiler_params=_cparams(("parallel", "parallel", "arbitrary")),
        name="diff_attention",
    )(*args)


GATE_G, GATE_BETA, GATE_TOT = 0, 2 * H_B, 4 * H_B
NEG_BIG = -1e30


def _gates_kernel(ab_ref, alog_ref, dtb_ref, o_ref, *, seq):
    n_chunks = seq // CHUNK
    ab = ab_ref[...]
    lane = lax.broadcasted_iota(jnp.int32, ab.shape, 1)
    pos = lax.broadcasted_iota(jnp.int32, ab.shape, 0) % CHUNK
    g = -jnp.exp(alog_ref[...]) * _softplus(ab + dtb_ref[...])
    g = jnp.where(lane < 2 * H_B, g, 0.0)
    fwd = g
    bwd = g
    sft = 1
    while sft < CHUNK:
        fwd = fwd + jnp.where(pos >= sft, pltpu.roll(fwd, sft, 0), 0.0)
        bwd = bwd + jnp.where(pos < CHUNK - sft, pltpu.roll(bwd, seq - sft, 0), 0.0)
        sft *= 2
    cs = jnp.where(lane < H_B, fwd, bwd)
    tot = jnp.sum(g.reshape(n_chunks, CHUNK, LANES), axis=1, keepdims=True)
    tot = jnp.broadcast_to(tot, (n_chunks, CHUNK, LANES)).reshape(seq, LANES)
    tot = pltpu.roll(tot, GATE_TOT, 1)
    out = jnp.where(lane < GATE_BETA, cs, jnp.where(lane < GATE_TOT, _sigmoid(ab), tot))
    o_ref[...] = out


def _delta_gates(tail3, alog_row, dtb_row):
    bsz, seq, _ = tail3.shape
    return pl.pallas_call(
        functools.partial(_gates_kernel, seq=seq),
        grid=(bsz,),
        in_specs=[pl.BlockSpec((None, seq, LANES), lambda b: (b, 0, 0)),
                  pl.BlockSpec((1, LANES), lambda b: (0, 0)),
                  pl.BlockSpec((1, LANES), lambda b: (0, 0))],
        out_specs=pl.BlockSpec((None, seq, LANES), lambda b: (b, 0, 0)),
        out_shape=jax.ShapeDtypeStruct((bsz, seq, LANES), F32),
        compiler_params=_cparams(("parallel",)),
        name="delta_gates",
    )(tail3, alog_row, dtb_row)


PAD_ROWS = 8


def _conv_silu(x_ref, cols, w, pad_ref):
    seq = x_ref.shape[0]
    pad_ref[PAD_ROWS:PAD_ROWS + seq, :] = x_ref[:, cols]
    y = None
    for t in range(CONV_K):
        s = PAD_ROWS + t - CONV_K // 2
        term = pad_ref[s:s + seq, :] * w[t:t + 1]
        y = term if y is None else y + term
    return _silu(y)


def _lane_col(x, col):
    lane = lax.broadcasted_iota(jnp.int32, x.shape, 1)
    c = jnp.sum(jnp.where(lane == col, x, 0.0), axis=-1, keepdims=True)
    return jnp.broadcast_to(c, x.shape)


def _delta_kernel(*refs, seq, hps, group, has_s0, emit_state):
    it = iter(refs)
    qr_ref, kr_ref, vr_ref, gate_ref, tok_ref = (next(it) for _ in range(5))
    wq_ref, wk_ref, wv_ref, nw_ref = (next(it) for _ in range(4))
    s0_ref = next(it) if has_s0 else None
    o_ref = next(it)
    st_ref = next(it) if emit_state else None
    kq_s, rhs_s, qg_s, akg_s, g_s, beta_s, egl_s, lhs_s, add_s, oacc_s, s_s, pad_s = it

    hg = pl.program_id(1)
    n_chunks = seq // CHUNK
    n_inst = hps * n_chunks
    left = lax.broadcasted_iota(jnp.int32, (seq, LANES), 1) < CHUNK
    tok = tok_ref[...]
    pad_s[0:PAD_ROWS, :] = jnp.zeros((PAD_ROWS, LANES), F32)
    pad_s[PAD_ROWS + seq:2 * PAD_ROWS + seq, :] = jnp.zeros((PAD_ROWS, LANES), F32)

    def phase_a(hh):
        cols = slice(hh * LANES, (hh + 1) * LANES)
        head = hg * hps + hh
        inst = slice(hh * n_chunks, (hh + 1) * n_chunks)
        rows = slice(hh * seq, (hh + 1) * seq)
        q = _conv_silu(qr_ref, cols, wq_ref[:, cols], pad_s)
        k = _conv_silu(kr_ref, cols, wk_ref[:, cols], pad_s)
        v = _conv_silu(vr_ref, cols, wv_ref[:, cols], pad_s)
        q = q * lax.rsqrt(jnp.sum(q * q, axis=-1, keepdims=True) + EPS) * (DK_B ** -0.5)
        k = k * lax.rsqrt(jnp.sum(k * k, axis=-1, keepdims=True) + EPS)
        kq_s[inst, 0:CHUNK, :] = k.astype(BF16).reshape(n_chunks, CHUNK, LANES)
        kq_s[inst, CHUNK:2 * CHUNK, :] = q.astype(BF16).reshape(n_chunks, CHUNK, LANES)
        packed = []
        for d in range(2):
            cs = _lane_col(tok, GATE_G + d * H_B + head)
            beta = _lane_col(tok, GATE_BETA + d * H_B + head)
            tot = _lane_col(tok, GATE_TOT + d * H_B + head)
            eg = jnp.exp(cs)
            drows = slice(d * CHUNK, (d + 1) * CHUNK)
            rhs_s[inst, drows, 0:LANES] = (v * beta).astype(BF16).reshape(n_chunks, CHUNK, LANES)
            rhs_s[inst, drows, LANES:2 * LANES] = (k * (beta * eg)).astype(BF16).reshape(n_chunks, CHUNK, LANES)
            qg_s[d, inst] = (q * eg).reshape(n_chunks, CHUNK, LANES)
            kg = (k * jnp.exp(tot - cs)).reshape(n_chunks, CHUNK, LANES)
            packed.append((cs, beta, kg))
            egl_s[d, inst] = jnp.exp(tot.reshape(n_chunks, CHUNK, LANES)[:, 0:8, :])
        g_s[rows, :] = jnp.where(left, packed[0][0], packed[1][0])
        beta_s[rows, :] = jnp.where(left, packed[0][1], packed[1][1])
        for c in range(n_chunks):
            kg_pair = jnp.concatenate([packed[0][2][c], packed[1][2][c]], axis=0)
            akg_s[hh * n_chunks + c, CHUNK:CHUNK + DK_B, :] = kg_pair.T.astype(BF16)

    ri = lax.broadcasted_iota(jnp.int32, (CHUNK, LANES), 0)
    cj = lax.broadcasted_iota(jnp.int32, (CHUNK, LANES), 1)
    left2 = cj < CHUNK
    cj = jnp.where(left2, cj, cj - CHUNK)
    diag2 = (ri == cj)[None]
    eye2 = (ri == cj).astype(F32)[None]
    incl2 = (jnp.where(left2, ri - cj, cj - ri) >= 0)[None]
    keep_l = left2[0:1].astype(F32).astype(BF16)[None]
    keep_r = (1.0 - left2[0:1].astype(F32)).astype(BF16)[None]

    def blockdiag(x):
        return jnp.concatenate([x * keep_l, x * keep_r], axis=1)

    def phase_b(i0):
        inst = slice(i0, i0 + group)
        rows = slice(i0 * CHUNK, (i0 + group) * CHUNK)
        kq = kq_s[inst]
        kc = kq[:, 0:CHUNK, :]
        kkqk = _bmm_nt(kq, jnp.concatenate([kc, kc], axis=1))
        kk, qk = kkqk[:, 0:CHUNK, :], kkqk[:, CHUNK:2 * CHUNK, :]
        gcol = g_s[rows, :].reshape(group, CHUNK, LANES)
        bcol = beta_s[rows, :].reshape(group, CHUNK, LANES)
        grow = jnp.sum(eye2 * gcol, axis=1, keepdims=True)
        decay = jnp.exp(jnp.where(incl2, gcol - grow, NEG_BIG))
        akg_s[inst, 0:CHUNK, :] = (qk * decay).astype(BF16)
        nm = jnp.where(diag2, 0.0, -(bcol * kk * decay))
        t = eye2 + nm
        nb = nm.astype(BF16)
        p = _bmm(nb, blockdiag(nb))
        for _ in range(4):
            pb = p.astype(BF16)
            tp = _bmm(jnp.concatenate([t.astype(BF16), pb], axis=1), blockdiag(pb))
            t = t + tp[:, 0:CHUNK, :]
            p = tp[:, CHUNK:2 * CHUNK, :]
        t = t + _bmm(t.astype(BF16), blockdiag(p.astype(BF16)))
        tb = t.astype(BF16)
        resid = (eye2 - t) + _bmm(nb, blockdiag(tb))
        t = t + _bmm(tb, blockdiag(resid.astype(BF16)))
        uw = _bmm(blockdiag(t.astype(BF16)), rhs_s[inst])
        uwb = uw.astype(BF16)
        akg = akg_s[inst]
        for d, keep in enumerate((keep_l, keep_r)):
            res = _bmm(akg * keep, uwb)
            lhs_s[d, inst, 0:CHUNK, :] = (qg_s[d, inst] - res[:, 0:CHUNK, LANES:2 * LANES]).astype(BF16)
            lhs_s[d, inst, CHUNK:CHUNK + DK_B, :] = (-res[:, CHUNK:CHUNK + DK_B, LANES:2 * LANES]).astype(BF16)
            add_s[d, inst] = res[:, :, 0:LANES]

    heads_per_group = max(1, group // n_chunks)
    for i0 in range(0, n_inst, group):
        first = i0 // n_chunks
        for hh in range(first, min(hps, first + heads_per_group)):
            phase_a(hh)
        phase_b(i0)

    for hh in range(hps):
        for d in range(2):
            s_s[d, hh] = s0_ref[d, hh] if has_s0 else jnp.zeros((DK_B, DV_B), F32)

    def scan_step(t, carry):
        for hh in range(hps):
            for d in range(2):
                c = hh * n_chunks + (t if d == 0 else n_chunks - 1 - t)
                rows = pl.ds(pl.multiple_of(c * CHUNK, CHUNK), CHUNK)
                s = s_s[d, hh]
                r = _dot(lhs_s[d, c], s.astype(BF16)) + add_s[d, c]
                oacc_s[d, rows, :] = r[0:CHUNK]
                s_s[d, hh] = egl_s[d, c][0:1, :] * s + r[CHUNK:CHUNK + DK_B]
        return carry

    lax.fori_loop(0, n_chunks, scan_step, 0)

    for hh in range(hps):
        cols = slice(hh * LANES, (hh + 1) * LANES)
        rows = slice(hh * seq, (hh + 1) * seq)
        if emit_state:
            st_ref[0, hh] = s_s[0, hh]
            st_ref[1, hh] = s_s[1, hh]
        o = oacc_s[0, rows, :] + oacc_s[1, rows, :]
        ms = jnp.mean(o * o, axis=-1, keepdims=True)
        o = o * lax.rsqrt(ms + EPS) * nw_ref[...] * _silu(gate_ref[:, cols])
        o_ref[:, cols] = o.astype(o_ref.dtype)


def _delta_mixer(proj3, tok3, conv_w, norm_w, s0, emit_state):
    bsz, seq, _ = proj3.shape
    hps = max(2, min(H_B, 2048 // seq))
    has_s0 = s0 is not None
    n_chunks = seq // CHUNK
    n_inst = hps * n_chunks
    group = min(n_inst, 16)
    wide = hps * LANES

    def col(cb):
        return pl.BlockSpec((None, seq, wide), lambda b, g: (b, 0, cb // hps + g))

    def wcol(cb):
        return pl.BlockSpec((CONV_K, wide), lambda b, g: (0, cb // hps + g))

    in_specs = [col(COL_DQ), col(COL_DK), col(COL_DV), col(COL_GATE),
                pl.BlockSpec((None, seq, LANES), lambda b, g: (b, 0, 0)),
                wcol(0), wcol(H_B), wcol(2 * H_B),
                pl.BlockSpec((1, DV_B), lambda b, g: (0, 0))]
    args = [proj3, proj3, proj3, proj3, tok3, conv_w, conv_w, conv_w, norm_w]
    state_spec = pl.BlockSpec((None, 2, hps, DK_B, DV_B), lambda b, g: (b, 0, g, 0, 0))
    if has_s0:
        in_specs.append(state_spec)
        args.append(s0)
    out_specs = [pl.BlockSpec((None, seq, wide), lambda b, g: (b, 0, g))]
    out_shape = [jax.ShapeDtypeStruct((bsz, seq, H_B * DV_B), BF16)]
    if emit_state:
        out_specs.append(state_spec)
        out_shape.append(jax.ShapeDtypeStruct((bsz, 2, H_B, DK_B, DV_B), F32))
    scratch = [
        pltpu.VMEM((n_inst, 2 * CHUNK, LANES), BF16),
        pltpu.VMEM((n_inst, 2 * CHUNK, 2 * LANES), BF16),
        pltpu.VMEM((2, n_inst, CHUNK, LANES), F32),
        pltpu.VMEM((n_inst, CHUNK + DK_B, LANES), BF16),
        pltpu.VMEM((hps * seq, LANES), F32),
        pltpu.VMEM((hps * seq, LANES), F32),
        pltpu.VMEM((2, n_inst, 8, LANES), F32),
        pltpu.VMEM((2, n_inst, CHUNK + DK_B, LANES), BF16),
        pltpu.VMEM((2, n_inst, CHUNK + DK_B, LANES), F32),
        pltpu.VMEM((2, hps * seq, LANES), F32),
        pltpu.VMEM((2, hps, DK_B, DV_B), F32),
        pltpu.VMEM((seq + 2 * PAD_ROWS, LANES), F32),
    ]
    outs = pl.pallas_call(
        functools.partial(_delta_kernel, seq=seq, hps=hps, group=group, has_s0=has_s0,
                          emit_state=emit_state),
        grid=(bsz, H_B // hps),
        in_specs=in_specs,
        out_specs=out_specs,
        out_shape=out_shape,
        scratch_shapes=scratch,
        compiler_params=_cparams(("parallel", "parallel")),
        name="delta_mixer",
    )(*args)
    return outs if emit_state else (outs[0], None)


def _outproj_kernel(x_ref, oa_ref, ob_ref, g1_ref, sh_ref, sc_ref, n2_ref, wa_ref, wb_ref, h_ref, hn_ref):
    mixed = _dot(oa_ref[...], wa_ref[...]) + _dot(ob_ref[...], wb_ref[...])
    h = x_ref[...] + g1_ref[...] * mixed
    h_ref[...] = h
    ms = jnp.mean(h * h, axis=-1, keepdims=True)
    y = h * lax.rsqrt(ms + EPS) * n2_ref[...]
    hn_ref[...] = (y * (1.0 + sc_ref[...]) + sh_ref[...]).astype(BF16)


def _out_projection(x2d, oa2d, ob2d, mod4, mod_row, norm2, w_out):
    m = x2d.shape[0]
    tm = OUTPROJ_TILE
    half = w_out.shape[0] // 2

    def mod_spec(k):
        return pl.BlockSpec((None, None, 1, D_MODEL), lambda i: (mod_row(i, tm), k, 0, 0))

    def weight_spec(part):
        return pl.BlockSpec((half, D_MODEL), lambda i: (part, 0), pipeline_mode=pl.Buffered(1))

    return pl.pallas_call(
        _outproj_kernel,
        grid=(m // tm,),
        in_specs=[pl.BlockSpec((tm, D_MODEL), lambda i: (i, 0)),
                  pl.BlockSpec((tm, half), lambda i: (i, 0)),
                  pl.BlockSpec((tm, half), lambda i: (i, 0)),
                  mod_spec(2), mod_spec(3), mod_spec(4),
                  pl.BlockSpec((1, D_MODEL), lambda i: (0, 0)),
                  weight_spec(0), weight_spec(1)],
        out_specs=[pl.BlockSpec((tm, D_MODEL), lambda i: (i, 0)),
                   pl.BlockSpec((tm, D_MODEL), lambda i: (i, 0))],
        out_shape=[jax.ShapeDtypeStruct((m, D_MODEL), F32),
                   jax.ShapeDtypeStruct((m, D_MODEL), BF16)],
        compiler_params=_cparams(("parallel",)),
        name="out_projection",
    )(x2d, oa2d, ob2d, mod4, mod4, mod4, norm2, w_out, w_out)


def _mlp_kernel(hn_ref, h_ref, g2_ref, nf_ref, w1_ref, w2_ref, y_ref):
    f = pl.program_id(1)

    @pl.when(f == 0)
    def _():
        y_ref[...] = jnp.zeros_like(y_ref)

    a = jnp.maximum(_dot(hn_ref[...], w1_ref[...]), 0.0)
    a2 = (a * a).astype(BF16)
    for n0 in range(0, D_MODEL, MLP_OUT_CHUNK):
        sl = slice(n0, n0 + MLP_OUT_CHUNK)
        y_ref[:, sl] += _dot(a2, w2_ref[:, sl])

    @pl.when(f == pl.num_programs(1) - 1)
    def _():
        y = h_ref[...] + g2_ref[...] * y_ref[...]
        ms = jnp.mean(y * y, axis=-1, keepdims=True)
        y_ref[...] = y * lax.rsqrt(ms + EPS) * nf_ref[...]


def _mlp(hn2d, h2d, mod4, mod_row, norm_f, w1, w2, tf=512):
    m = h2d.shape[0]
    tm = TOKEN_TILE
    row_tile = pl.BlockSpec((tm, D_MODEL), lambda i, f: (i, 0), pipeline_mode=pl.Buffered(1))
    return pl.pallas_call(
        _mlp_kernel,
        grid=(m // tm, D_FF // tf),
        in_specs=[row_tile, row_tile,
                  pl.BlockSpec((None, None, 1, D_MODEL), lambda i, f: (mod_row(i, tm), 5, 0, 0)),
                  pl.BlockSpec((1, D_MODEL), lambda i, f: (0, 0)),
                  pl.BlockSpec((D_MODEL, tf), lambda i, f: (0, f)),
                  pl.BlockSpec((tf, D_MODEL), lambda i, f: (f, 0))],
        out_specs=pl.BlockSpec((tm, D_MODEL), lambda i, f: (i, 0)),
        out_shape=jax.ShapeDtypeStruct((m, D_MODEL), F32),
        compiler_params=_cparams(("parallel", "arbitrary")),
        name="mlp",
    )(hn2d, h2d, mod4, norm_f, w1, w2)


def _trunk(x, mod4, mod_row, params, ctx):
    (w_main, w_tail, conv_w, alog_row, dtb_row, delta_norm, attn_lambda, attn_subln, norm1,
     w_out, norm2, w_ff1, w_ff2, norm_f) = params
    bsz, seq, _ = x.shape
    x2d = x.reshape(bsz * seq, D_MODEL)
    if ctx is None:
        q, new_k, new_v, main, tail = _in_projection(x2d, mod4, mod_row, norm1, w_main, w_tail, None, seq)
        oa = _diff_attention(q.reshape(bsz, seq, -1), new_k, new_v, None, None, attn_lambda, attn_subln,
                             tq=seq)
    else:
        ctx_kt, ctx_v, s0 = ctx
        new_k = new_v = None
        q, k_t, v, main, tail = _in_projection(x2d, mod4, mod_row, norm1, w_main, w_tail,
                                               _rope_tables(seq), seq)
        oa = _diff_attention(q.reshape(bsz, seq, -1), k_t, v.reshape(bsz, seq, -1), ctx_kt, ctx_v,
                             attn_lambda, attn_subln, tq=ATTN_Q_TILE)
    main3 = main.reshape(bsz, seq, N_MAIN - ATTN_COLS)
    tok3 = _delta_gates(tail.reshape(bsz, seq, LANES), alog_row, dtb_row)
    ob, state = _delta_mixer(main3, tok3, conv_w, delta_norm, None if ctx is None else s0, ctx is None)
    h, hn = _out_projection(x2d, oa.reshape(bsz * seq, -1), ob.reshape(bsz * seq, -1), mod4, mod_row,
                            norm2, w_out)
    y = _mlp(hn, h, mod4, mod_row, norm_f, w_ff1, w_ff2)
    return y.reshape(bsz, seq, D_MODEL), new_k, new_v, state


def kernel(x_prompt, x_sample, cache_attn_k, cache_attn_v, state_delta, c, c_ctx, w_mod, b_mod, norm1, w_in, conv_w, a_log, dt_bias, delta_norm, attn_lambda, attn_subln, w_out, norm2, w_ff1, w_ff2, norm_f):
    bp, lp, _ = x_prompt.shape
    bs, ls, _ = x_sample.shape
    past = cache_attn_k.shape[3]

    rows = 16
    cc = jnp.concatenate([c_ctx[None, :], c, jnp.zeros((rows - 1 - bs, D_MODEL), F32)], axis=0)
    mod = _modulation(cc, w_mod[0], b_mod[0][None, :])
    mod4 = mod.reshape(rows, N_MOD, 1, D_MODEL)

    w_main = w_in[0].astype(BF16)
    w_tail = jnp.pad(w_in[0][:, N_MAIN:], ((0, 0), (0, LANES - N_TAIL))).astype(BF16)
    alog_row = jnp.pad(a_log[0].reshape(1, 2 * H_B), ((0, 0), (0, LANES - 2 * H_B)))
    dtb_row = jnp.pad(dt_bias[0].reshape(1, 2 * H_B), ((0, 0), (0, LANES - 2 * H_B)))
    params = (w_main, w_tail, conv_w[0], alog_row, dtb_row, delta_norm[0][None, :], attn_lambda[0],
              attn_subln[0][None, :], norm1[0][None, :], w_out[0].astype(BF16), norm2[0][None, :],
              w_ff1[0].astype(BF16), w_ff2[0].astype(BF16), norm_f[None, :])

    y_prompt, new_k, new_v, state = _trunk(x_prompt, mod4, lambda i, tm: 0, params, None)
    ctx_kt = cache_attn_k[:, 0].reshape(bs, H_A, past, 2 * HD_A).astype(BF16).transpose(0, 1, 3, 2)
    ctx = (ctx_kt, cache_attn_v[:, 0].astype(BF16), state_delta[:, 0])
    y_sample, _, _, _ = _trunk(x_sample, mod4, lambda i, tm: 1 + (i * tm) // ls, params, ctx)

    new_k = new_k.reshape(bp, 1, H_A, lp, 2, HD_A)
    new_v = new_v.reshape(bp, 1, H_A, lp, DV_A)
    return (y_prompt, y_sample, new_k, new_v, state[:, None])
```

```python
import functools
import math

import jax
import jax.numpy as jnp
from jax import lax
from jax.experimental import pallas as pl
from jax.experimental.pallas import tpu as pltpu

F32 = jnp.float32
BF16 = jnp.bfloat16

D_MODEL = 2048
GRID_W = 64
H_A = 8
HD_A = 64
DV_A = 128
H_B = 8
DK_B = 128
DV_B = 128
CONV_K = 5
CHUNK = 64
D_FF = 4 * D_MODEL
ROPE_BASE = 10000.0
EPS = 1e-6
N_MOD = 6
LAM_INIT = 0.8 - 0.6 * math.exp(-0.3 * 0)

LANES = 128
N_MAIN = 7168
N_TAIL = 32
ATTN_COLS = 3 * H_A * DV_A
PROJ_TILE = 512
ATTN_TILES = ATTN_COLS // PROJ_TILE
Q_SCALE = HD_A ** -0.5 * math.log2(math.e)
COL_DQ, COL_DK, COL_DV, COL_GATE = 0, 8, 16, 24
VMEM_LIMIT = 56 * 1024 * 1024
TOKEN_TILE = 1024
OUTPROJ_TILE = 512
MLP_OUT_CHUNK = 512
ATTN_Q_TILE = 1024
ATTN_ROW_BLOCK = 256
ATTN_CTX_HEADS = 4
ATTN_LATENT_HEADS = 2


def _cparams(sem):
    return pltpu.CompilerParams(dimension_semantics=sem, vmem_limit_bytes=VMEM_LIMIT)


def _sigmoid(x):
    return 0.5 * jnp.tanh(0.5 * x) + 0.5


def _silu(x):
    half = 0.5 * x
    return half * jnp.tanh(half) + half


def _softplus(x):
    return jnp.maximum(x, 0.0) + jnp.log(1.0 + jnp.exp(-jnp.abs(x)))


def _dot(a, b):
    return jnp.dot(a, b, preferred_element_type=F32)


def _bmm(a, b):
    return lax.dot_general(a, b, (((2,), (1,)), ((0,), (0,))), preferred_element_type=F32)


def _bmm_nt(a, b):
    return lax.dot_general(a, b, (((2,), (2,)), ((0,), (0,))), preferred_element_type=F32)


def _mod_kernel(c_ref, w_ref, b_ref, o_ref):
    s = _silu(c_ref[...]).astype(BF16)
    o_ref[...] = _dot(s, w_ref[...].astype(BF16)) + b_ref[...]


def _modulation(cc, w_mod, b_mod):
    rows = cc.shape[0]
    n = w_mod.shape[1]
    tn = 512
    return pl.pallas_call(
        _mod_kernel,
        grid=(n // tn,),
        in_specs=[pl.BlockSpec((rows, D_MODEL), lambda j: (0, 0)),
                  pl.BlockSpec((D_MODEL, tn), lambda j: (0, j)),
                  pl.BlockSpec((1, tn), lambda j: (0, j))],
        out_specs=pl.BlockSpec((rows, tn), lambda j: (0, j)),
        out_shape=jax.ShapeDtypeStruct((rows, n), F32),
        compiler_params=_cparams(("arbitrary",)),
        name="modulation",
    )(cc, w_mod, b_mod)


def _rope(x, cos, sin_lo, sin_hi):
    return (x * cos + pltpu.roll(x, LANES - 16, 1) * sin_lo + pltpu.roll(x, 16, 1) * sin_hi)


def _inproj_norm(x_ref, sh_ref, sc_ref, n1_ref, wt_ref, ot_ref, xn_ref):
    x = x_ref[...]
    ms = jnp.mean(x * x, axis=-1, keepdims=True)
    y = x * lax.rsqrt(ms + EPS) * n1_ref[...]
    y = y * (1.0 + sc_ref[...]) + sh_ref[...]
    xn = y.astype(BF16)
    xn_ref[...] = xn
    ot_ref[...] = _dot(xn, wt_ref[...])


def _inproj_latent_kernel(x_ref, sh_ref, sc_ref, n1_ref, w_ref, wt_ref, cos_ref, slo_ref, shi_ref,
                          q_ref, kt_ref, v_ref, main_ref, ot_ref, xn_ref):
    j = pl.program_id(1)
    q_tiles = ATTN_TILES // 3

    @pl.when(j == 0)
    def _():
        _inproj_norm(x_ref, sh_ref, sc_ref, n1_ref, wt_ref, ot_ref, xn_ref)

    def roped_heads():
        cos, slo, shi = cos_ref[...], slo_ref[...], shi_ref[...]
        half = PROJ_TILE // 2
        parts = [_dot(xn_ref[...], w_ref[:, c:c + half]) for c in range(0, PROJ_TILE, half)]
        for p, r in enumerate(parts):
            for g in range(half // LANES):
                yield p * (half // LANES) + g, _rope(r[:, g * LANES:(g + 1) * LANES], cos, slo, shi)

    @pl.when(j < q_tiles)
    def _():
        for g, x in roped_heads():
            q_ref[:, g * LANES:(g + 1) * LANES] = (x * Q_SCALE).astype(BF16)

    @pl.when((j >= q_tiles) & (j < 2 * q_tiles))
    def _():
        for g, x in roped_heads():
            kt_ref[g] = x.T.astype(BF16)

    @pl.when((j >= 2 * q_tiles) & (j < ATTN_TILES))
    def _():
        v_ref[...] = _dot(xn_ref[...], w_ref[...]).astype(BF16)

    @pl.when(j >= ATTN_TILES)
    def _():
        main_ref[...] = _dot(xn_ref[...], w_ref[...])


def _inproj_context_kernel(x_ref, sh_ref, sc_ref, n1_ref, w_ref, wt_ref,
                           q_ref, k_ref, v_ref, main_ref, ot_ref, xn_ref, *, seq):
    j = pl.program_id(1)
    q_tiles = ATTN_TILES // 3

    @pl.when(j == 0)
    def _():
        _inproj_norm(x_ref, sh_ref, sc_ref, n1_ref, wt_ref, ot_ref, xn_ref)

    def product():
        return _dot(xn_ref[...], w_ref[...])

    def store_heads(dst_ref):
        r = product()
        for g in range(PROJ_TILE // LANES):
            dst_ref[:, g] = r[:, g * LANES:(g + 1) * LANES].reshape(-1, seq, LANES)

    @pl.when(j < q_tiles)
    def _():
        q_ref[...] = (product() * Q_SCALE).astype(BF16)

    @pl.when((j >= q_tiles) & (j < 2 * q_tiles))
    def _():
        store_heads(k_ref)

    @pl.when((j >= 2 * q_tiles) & (j < ATTN_TILES))
    def _():
        store_heads(v_ref)

    @pl.when(j >= ATTN_TILES)
    def _():
        main_ref[...] = product()


def _in_projection(x2d, mod4, mod_row, norm1, w_main, w_tail, rope_tabs, seq):
    m = x2d.shape[0]
    tm, tn = TOKEN_TILE, PROJ_TILE
    n_delta = N_MAIN - ATTN_COLS
    in_specs = [
        pl.BlockSpec((tm, D_MODEL), lambda i, j: (i, 0)),
        pl.BlockSpec((None, None, 1, D_MODEL), lambda i, j: (mod_row(i, tm), 0, 0, 0)),
        pl.BlockSpec((None, None, 1, D_MODEL), lambda i, j: (mod_row(i, tm), 1, 0, 0)),
        pl.BlockSpec((1, D_MODEL), lambda i, j: (0, 0)),
        pl.BlockSpec((D_MODEL, tn), lambda i, j: (0, j)),
        pl.BlockSpec((D_MODEL, LANES), lambda i, j: (0, 0)),
    ]
    args = [x2d, mod4, mod4, norm1, w_main, w_tail]
    main_spec = pl.BlockSpec((tm, tn), lambda i, j: (i, jnp.maximum(j - ATTN_TILES, 0)))
    main_shape = jax.ShapeDtypeStruct((m, n_delta), F32)
    tail_spec = pl.BlockSpec((tm, LANES), lambda i, j: (i, 0))
    tail_shape = jax.ShapeDtypeStruct((m, LANES), F32)
    q_tiles = ATTN_TILES // 3
    heads_per_tile = tn // LANES

    def cols_spec(first_tile):
        return pl.BlockSpec((tm, tn), lambda i, j: (i, jnp.clip(j - first_tile, 0, q_tiles - 1)))

    cols_shape = jax.ShapeDtypeStruct((m, ATTN_COLS // 3), BF16)
    if rope_tabs is not None:
        assert tm == seq
        for t in rope_tabs:
            in_specs.append(pl.BlockSpec((tm, LANES), lambda i, j: (0, 0)))
            args.append(t)
        body = _inproj_latent_kernel
        kt_spec = pl.BlockSpec((None, heads_per_tile, LANES, seq),
                               lambda i, j: (i, jnp.clip(j - q_tiles, 0, q_tiles - 1), 0, 0))
        out_specs = [cols_spec(0), kt_spec, cols_spec(2 * q_tiles), main_spec, tail_spec]
        out_shape = [cols_shape, jax.ShapeDtypeStruct((m // seq, H_A, LANES, seq), BF16), cols_shape,
                     main_shape, tail_shape]
    else:
        bt = tm // seq

        def head_spec(first_tile):
            return pl.BlockSpec((bt, heads_per_tile, seq, LANES),
                                lambda i, j: (i, jnp.clip(j - first_tile, 0, q_tiles - 1), 0, 0))

        head_shape = jax.ShapeDtypeStruct((m // seq, H_A, seq, LANES), F32)
        body = functools.partial(_inproj_context_kernel, seq=seq)
        out_specs = [cols_spec(0), head_spec(q_tiles), head_spec(2 * q_tiles), main_spec, tail_spec]
        out_shape = [cols_shape, head_shape, head_shape, main_shape, tail_shape]
    return pl.pallas_call(
        body,
        grid=(m // tm, N_MAIN // tn),
        in_specs=in_specs,
        out_specs=out_specs,
        out_shape=out_shape,
        scratch_shapes=[pltpu.VMEM((tm, D_MODEL), BF16)],
        compiler_params=_cparams(("parallel", "arbitrary")),
        name="in_projection",
    )(*args)


def _rope_tables(seq_len):
    lane = jnp.arange(LANES)
    r = lane % HD_A
    axis = r // (HD_A // 2)
    u = r % (HD_A // 2)
    nf = HD_A // 4
    f = u % nf
    hi_half = u // nf
    inv = 1.0 / (ROPE_BASE ** (f.astype(F32) * 2.0 / (HD_A // 2)))
    t = jnp.arange(seq_len)
    pos = jnp.where(axis[None, :] == 0, (t // GRID_W)[:, None], (t % GRID_W)[:, None]).astype(F32)
    ang = pos * inv[None, :]
    cos, sin = jnp.cos(ang), jnp.sin(ang)
    sin_lo = jnp.where(hi_half[None, :] == 0, -sin, 0.0)
    sin_hi = jnp.where(hi_half[None, :] == 1, sin, 0.0)
    return cos, sin_lo, sin_hi


def _attn_kernel(*refs, has_ctx):
    if has_ctx:
        q_ref, k_ref, v_ref, ck_ref, cv_ref, lam_ref, g_ref, o_ref = refs
    else:
        q_ref, k_ref, v_ref, lam_ref, g_ref, o_ref = refs
    lp = lam_ref[...]
    lam = (jnp.exp(jnp.sum(lp[0:1] * lp[1:2], axis=-1, keepdims=True))
           - jnp.exp(jnp.sum(lp[2:3] * lp[3:4], axis=-1, keepdims=True)) + LAM_INIT)
    first = (lax.broadcasted_iota(jnp.int32, (1, LANES), 1) < HD_A).astype(F32)
    keep = (first.astype(BF16), (1.0 - first).astype(BF16))
    n_heads = k_ref.shape[0]
    blocks = [(slice(r, r + ATTN_ROW_BLOCK), hh) for hh in range(n_heads)
              for r in range(0, q_ref.shape[0], ATTN_ROW_BLOCK)]

    def keys_values(hh):
        if has_ctx:
            return ([ck_ref[hh], k_ref[hh]], [cv_ref[hh], v_ref[:, hh * LANES:(hh + 1) * LANES]])
        return [k_ref[hh].T.astype(BF16)], [v_ref[hh].astype(BF16)]

    def scores(block):
        rws, hh = block
        q = q_ref[rws, hh * LANES:(hh + 1) * LANES]
        keys_t, _ = keys_values(hh)
        return [[_dot(q * km, kt) for kt in keys_t] for km in keep]

    def softmax_diff(sb):
        probs = []
        for s in sb:
            mx = functools.reduce(jnp.maximum, [jnp.max(x, axis=-1, keepdims=True) for x in s])
            e = [jnp.exp2(x - mx) for x in s]
            z = functools.reduce(jnp.add, [jnp.sum(x, axis=-1, keepdims=True) for x in e])
            probs.append((e, 1.0 / z))
        (e0, r0), (e1, r1) = probs
        r1 = r1 * lam
        return [(a * r0 - b * r1).astype(BF16) for a, b in zip(e0, e1)]

    def output(block, pd):
        rws, hh = block
        _, vals = keys_values(hh)
        o = functools.reduce(jnp.add, [_dot(p, vv) for p, vv in zip(pd, vals)])
        ms = jnp.mean(o * o, axis=-1, keepdims=True)
        o = o * lax.rsqrt(ms + EPS) * g_ref[...] * (1.0 - LAM_INIT)
        o_ref[rws, hh * LANES:(hh + 1) * LANES] = o.astype(o_ref.dtype)

    sc, pd = {}, {}
    for i in range(len(blocks) + 2):
        if i < len(blocks):
            sc[i] = scores(blocks[i])
        if 0 <= i - 1 < len(blocks):
            pd[i - 1] = softmax_diff(sc.pop(i - 1))
        if 0 <= i - 2 < len(blocks):
            output(blocks[i - 2], pd.pop(i - 2))


def _diff_attention(q3, k_src, v_src, ctx_kt, ctx_v, attn_lambda, subln, tq):
    bsz, seq, _ = q3.shape
    has_ctx = ctx_kt is not None
    hps = ATTN_LATENT_HEADS if has_ctx else ATTN_CTX_HEADS
    if has_ctx:
        k_spec = pl.BlockSpec((None, hps, LANES, seq), lambda b, h, i: (b, h, 0, 0))
        v_spec = pl.BlockSpec((None, seq, hps * LANES), lambda b, h, i: (b, 0, h))
    else:
        k_spec = v_spec = pl.BlockSpec((None, hps, seq, LANES), lambda b, h, i: (b, h, 0, 0))
    in_specs = [pl.BlockSpec((None, tq, hps * LANES), lambda b, h, i: (b, i, h)), k_spec, v_spec]
    args = [q3, k_src, v_src]
    if has_ctx:
        past = ctx_v.shape[2]
        in_specs += [pl.BlockSpec((None, hps, LANES, past), lambda b, h, i: (b, h, 0, 0)),
                     pl.BlockSpec((None, hps, past, LANES), lambda b, h, i: (b, h, 0, 0))]
        args += [ctx_kt, ctx_v]
    in_specs += [pl.BlockSpec((4, HD_A), lambda b, h, i: (0, 0)),
                 pl.BlockSpec((1, DV_A), lambda b, h, i: (0, 0))]
    args += [attn_lambda, subln]
    return pl.pallas_call(
        functools.partial(_attn_kernel, has_ctx=has_ctx),
        grid=(bsz, H_A // hps, seq // tq),
        in_specs=in_specs,
        out_specs=pl.BlockSpec((None, tq, hps * LANES), lambda b, h, i: (b, i, h)),
        out_shape=jax.ShapeDtypeStruct((bsz, seq, H_A * DV_A), BF16),
        compiler_params=_cparams(("parallel", "parallel", "arbitrary")),
        name="diff_attention",
    )(*args)


GATE_G, GATE_BETA, GATE_TOT = 0, 2 * H_B, 4 * H_B
NEG_BIG = -1e30


def _gates_kernel(ab_ref, alog_ref, dtb_ref, o_ref, *, seq):
    n_chunks = seq // CHUNK
    ab = ab_ref[...]
    lane = lax.broadcasted_iota(jnp.int32, ab.shape, 1)
    pos = lax.broadcasted_iota(jnp.int32, ab.shape, 0) % CHUNK
    g = -jnp.exp(alog_ref[...]) * _softplus(ab + dtb_ref[...])
    g = jnp.where(lane < 2 * H_B, g, 0.0)
    fwd = g
    bwd = g
    sft = 1
    while sft < CHUNK:
        fwd = fwd + jnp.where(pos >= sft, pltpu.roll(fwd, sft, 0), 0.0)
        bwd = bwd + jnp.where(pos < CHUNK - sft, pltpu.roll(bwd, seq - sft, 0), 0.0)
        sft *= 2
    cs = jnp.where(lane < H_B, fwd, bwd)
    tot = jnp.sum(g.reshape(n_chunks, CHUNK, LANES), axis=1, keepdims=True)
    tot = jnp.broadcast_to(tot, (n_chunks, CHUNK, LANES)).reshape(seq, LANES)
    tot = pltpu.roll(tot, GATE_TOT, 1)
    out = jnp.where(lane < GATE_BETA, cs, jnp.where(lane < GATE_TOT, _sigmoid(ab), tot))
    o_ref[...] = out


def _delta_gates(tail3, alog_row, dtb_row):
    bsz, seq, _ = tail3.shape
    return pl.pallas_call(
        functools.partial(_gates_kernel, seq=seq),
        grid=(bsz,),
        in_specs=[pl.BlockSpec((None, seq, LANES), lambda b: (b, 0, 0)),
                  pl.BlockSpec((1, LANES), lambda b: (0, 0)),
                  pl.BlockSpec((1, LANES), lambda b: (0, 0))],
        out_specs=pl.BlockSpec((None, seq, LANES), lambda b: (b, 0, 0)),
        out_shape=jax.ShapeDtypeStruct((bsz, seq, LANES), F32),
        compiler_params=_cparams(("parallel",)),
        name="delta_gates",
    )(tail3, alog_row, dtb_row)


PAD_ROWS = 8


def _conv_silu(x_ref, cols, w, pad_ref):
    seq = x_ref.shape[0]
    pad_ref[PAD_ROWS:PAD_ROWS + seq, :] = x_ref[:, cols]
    y = None
    for t in range(CONV_K):
        s = PAD_ROWS + t - CONV_K // 2
        term = pad_ref[s:s + seq, :] * w[t:t + 1]
        y = term if y is None else y + term
    return _silu(y)


def _lane_col(x, col):
    lane = lax.broadcasted_iota(jnp.int32, x.shape, 1)
    c = jnp.sum(jnp.where(lane == col, x, 0.0), axis=-1, keepdims=True)
    return jnp.broadcast_to(c, x.shape)


def _delta_kernel(*refs, seq, hps, group, has_s0, emit_state):
    it = iter(refs)
    qr_ref, kr_ref, vr_ref, gate_ref, tok_ref = (next(it) for _ in range(5))
    wq_ref, wk_ref, wv_ref, nw_ref = (next(it) for _ in range(4))
    s0_ref = next(it) if has_s0 else None
    o_ref = next(it)
    st_ref = next(it) if emit_state else None
    kq_s, rhs_s, qg_s, akg_s, g_s, beta_s, egl_s, lhs_s, add_s, oacc_s, s_s, pad_s = it

    hg = pl.program_id(1)
    n_chunks = seq // CHUNK
    n_inst = hps * n_chunks
    left = lax.broadcasted_iota(jnp.int32, (seq, LANES), 1) < CHUNK
    tok = tok_ref[...]
    pad_s[0:PAD_ROWS, :] = jnp.zeros((PAD_ROWS, LANES), F32)
    pad_s[PAD_ROWS + seq:2 * PAD_ROWS + seq, :] = jnp.zeros((PAD_ROWS, LANES), F32)

    def phase_a(hh):
        cols = slice(hh * LANES, (hh + 1) * LANES)
        head = hg * hps + hh
        inst = slice(hh * n_chunks, (hh + 1) * n_chunks)
        rows = slice(hh * seq, (hh + 1) * seq)
        q = _conv_silu(qr_ref, cols, wq_ref[:, cols], pad_s)
        k = _conv_silu(kr_ref, cols, wk_ref[:, cols], pad_s)
        v = _conv_silu(vr_ref, cols, wv_ref[:, cols], pad_s)
        q = q * lax.rsqrt(jnp.sum(q * q, axis=-1, keepdims=True) + EPS) * (DK_B ** -0.5)
        k = k * lax.rsqrt(jnp.sum(k * k, axis=-1, keepdims=True) + EPS)
        kq_s[inst, 0:CHUNK, :] = k.astype(BF16).reshape(n_chunks, CHUNK, LANES)
        kq_s[inst, CHUNK:2 * CHUNK, :] = q.astype(BF16).reshape(n_chunks, CHUNK, LANES)
        packed = []
        for d in range(2):
            cs = _lane_col(tok, GATE_G + d * H_B + head)
            beta = _lane_col(tok, GATE_BETA + d * H_B + head)
            tot = _lane_col(tok, GATE_TOT + d * H_B + head)
            eg = jnp.exp(cs)
            drows = slice(d * CHUNK, (d + 1) * CHUNK)
            rhs_s[inst, drows, 0:LANES] = (v * beta).astype(BF16).reshape(n_chunks, CHUNK, LANES)
            rhs_s[inst, drows, LANES:2 * LANES] = (k * (beta * eg)).astype(BF16).reshape(n_chunks, CHUNK, LANES)
            qg_s[d, inst] = (q * eg).reshape(n_chunks, CHUNK, LANES)
            kg = (k * jnp.exp(tot - cs)).reshape(n_chunks, CHUNK, LANES)
            packed.append((cs, beta, kg))
            egl_s[d, inst] = jnp.exp(tot.reshape(n_chunks, CHUNK, LANES)[:, 0:8, :])
        g_s[rows, :] = jnp.where(left, packed[0][0], packed[1][0])
        beta_s[rows, :] = jnp.where(left, packed[0][1], packed[1][1])
        for c in range(n_chunks):
            kg_pair = jnp.concatenate([packed[0][2][c], packed[1][2][c]], axis=0)
            akg_s[hh * n_chunks + c, CHUNK:CHUNK + DK_B, :] = kg_pair.T.astype(BF16)

    ri = lax.broadcasted_iota(jnp.int32, (CHUNK, LANES), 0)
    cj = lax.broadcasted_iota(jnp.int32, (CHUNK, LANES), 1)
    left2 = cj < CHUNK
    cj = jnp.where(left2, cj, cj - CHUNK)
    diag2 = (ri == cj)[None]
    eye2 = (ri == cj).astype(F32)[None]
    incl2 = (jnp.where(left2, ri - cj, cj - ri) >= 0)[None]
    keep_l = left2[0:1].astype(F32).astype(BF16)[None]
    keep_r = (1.0 - left2[0:1].astype(F32)).astype(BF16)[None]

    def blockdiag(x):
        return jnp.concatenate([x * keep_l, x * keep_r], axis=1)

    def phase_b(i0):
        inst = slice(i0, i0 + group)
        rows = slice(i0 * CHUNK, (i0 + group) * CHUNK)
        kq = kq_s[inst]
        kc = kq[:, 0:CHUNK, :]
        kkqk = _bmm_nt(kq, jnp.concatenate([kc, kc], axis=1))
        kk, qk = kkqk[:, 0:CHUNK, :], kkqk[:, CHUNK:2 * CHUNK, :]
        gcol = g_s[rows, :].reshape(group, CHUNK, LANES)
        bcol = beta_s[rows, :].reshape(group, CHUNK, LANES)
        grow = jnp.sum(eye2 * gcol, axis=1, keepdims=True)
        decay = jnp.exp(jnp.where(incl2, gcol - grow, NEG_BIG))
        akg_s[inst, 0:CHUNK, :] = (qk * decay).astype(BF16)
        nm = jnp.where(diag2, 0.0, -(bcol * kk * decay))
        t = eye2 + nm
        nb = nm.astype(BF16)
        p = _bmm(nb, blockdiag(nb))
        for _ in range(4):
            pb = p.astype(BF16)
            tp = _bmm(jnp.concatenate([t.astype(BF16), pb], axis=1), blockdiag(pb))
            t = t + tp[:, 0:CHUNK, :]
            p = tp[:, CHUNK:2 * CHUNK, :]
        t = t + _bmm(t.astype(BF16), blockdiag(p.astype(BF16)))
        tb = t.astype(BF16)
        resid = (eye2 - t) + _bmm(nb, blockdiag(tb))
        t = t + _bmm(tb, blockdiag(resid.astype(BF16)))
        uw = _bmm(blockdiag(t.astype(BF16)), rhs_s[inst])
        uwb = uw.astype(BF16)
        akg = akg_s[inst]
        for d, keep in enumerate((keep_l, keep_r)):
            res = _bmm(akg * keep, uwb)
            lhs_s[d, inst, 0:CHUNK, :] = (qg_s[d, inst] - res[:, 0:CHUNK, LANES:2 * LANES]).astype(BF16)
            lhs_s[d, inst, CHUNK:CHUNK + DK_B, :] = (-res[:, CHUNK:CHUNK + DK_B, LANES:2 * LANES]).astype(BF16)
            add_s[d, inst] = res[:, :, 0:LANES]

    heads_per_group = max(1, group // n_chunks)
    for i0 in range(0, n_inst, group):
        first = i0 // n_chunks
        for hh in range(first, min(hps, first + heads_per_group)):
            phase_a(hh)
        phase_b(i0)

    for hh in range(hps):
        for d in range(2):
            s_s[d, hh] = s0_ref[d, hh] if has_s0 else jnp.zeros((DK_B, DV_B), F32)

    for t in range(n_chunks):
        for hh in range(hps):
            for d in range(2):
                c = hh * n_chunks + (t if d == 0 else n_chunks - 1 - t)
                rows = slice(c * CHUNK, (c + 1) * CHUNK)
                s = s_s[d, hh]
                r = _dot(lhs_s[d, c], s.astype(BF16)) + add_s[d, c]
                oacc_s[d, rows, :] = r[0:CHUNK]
                s_s[d, hh] = egl_s[d, c][0:1, :] * s + r[CHUNK:CHUNK + DK_B]

    for hh in range(hps):
        cols = slice(hh * LANES, (hh + 1) * LANES)
        rows = slice(hh * seq, (hh + 1) * seq)
        if emit_state:
            st_ref[0, hh] = s_s[0, hh]
            st_ref[1, hh] = s_s[1, hh]
        o = oacc_s[0, rows, :] + oacc_s[1, rows, :]
        ms = jnp.mean(o * o, axis=-1, keepdims=True)
        o = o * lax.rsqrt(ms + EPS) * nw_ref[...] * _silu(gate_ref[:, cols])
        o_ref[:, cols] = o.astype(o_ref.dtype)


def _delta_mixer(proj3, tok3, conv_w, norm_w, s0, emit_state):
    bsz, seq, _ = proj3.shape
    hps = max(2, min(H_B, 2048 // seq))
    has_s0 = s0 is not None
    n_chunks = seq // CHUNK
    n_inst = hps * n_chunks
    group = min(n_inst, 16)
    wide = hps * LANES

    def col(cb):
        return pl.BlockSpec((None, seq, wide), lambda b, g: (b, 0, cb // hps + g))

    def wcol(cb):
        return pl.BlockSpec((CONV_K, wide), lambda b, g: (0, cb // hps + g))

    in_specs = [col(COL_DQ), col(COL_DK), col(COL_DV), col(COL_GATE),
                pl.BlockSpec((None, seq, LANES), lambda b, g: (b, 0, 0)),
                wcol(0), wcol(H_B), wcol(2 * H_B),
                pl.BlockSpec((1, DV_B), lambda b, g: (0, 0))]
    args = [proj3, proj3, proj3, proj3, tok3, conv_w, conv_w, conv_w, norm_w]
    state_spec = pl.BlockSpec((None, 2, hps, DK_B, DV_B), lambda b, g: (b, 0, g, 0, 0))
    if has_s0:
        in_specs.append(state_spec)
        args.append(s0)
    out_specs = [pl.BlockSpec((None, seq, wide), lambda b, g: (b, 0, g))]
    out_shape = [jax.ShapeDtypeStruct((bsz, seq, H_B * DV_B), BF16)]
    if emit_state:
        out_specs.append(state_spec)
        out_shape.append(jax.ShapeDtypeStruct((bsz, 2, H_B, DK_B, DV_B), F32))
    scratch = [
        pltpu.VMEM((n_inst, 2 * CHUNK, LANES), BF16),
        pltpu.VMEM((n_inst, 2 * CHUNK, 2 * LANES), BF16),
        pltpu.VMEM((2, n_inst, CHUNK, LANES), F32),
        pltpu.VMEM((n_inst, CHUNK + DK_B, LANES), BF16),
        pltpu.VMEM((hps * seq, LANES), F32),
        pltpu.VMEM((hps * seq, LANES), F32),
        pltpu.VMEM((2, n_inst, 8, LANES), F32),
        pltpu.VMEM((2, n_inst, CHUNK + DK_B, LANES), BF16),
        pltpu.VMEM((2, n_inst, CHUNK + DK_B, LANES), F32),
        pltpu.VMEM((2, hps * seq, LANES), F32),
        pltpu.VMEM((2, hps, DK_B, DV_B), F32),
        pltpu.VMEM((seq + 2 * PAD_ROWS, LANES), F32),
    ]
    outs = pl.pallas_call(
        functools.partial(_delta_kernel, seq=seq, hps=hps, group=group, has_s0=has_s0,
                          emit_state=emit_state),
        grid=(bsz, H_B // hps),
        in_specs=in_specs,
        out_specs=out_specs,
        out_shape=out_shape,
        scratch_shapes=scratch,
        compiler_params=_cparams(("parallel", "parallel")),
        name="delta_mixer",
    )(*args)
    return outs if emit_state else (outs[0], None)


def _outproj_kernel(x_ref, oa_ref, ob_ref, g1_ref, sh_ref, sc_ref, n2_ref, wa_ref, wb_ref, h_ref, hn_ref):
    mixed = _dot(oa_ref[...], wa_ref[...]) + _dot(ob_ref[...], wb_ref[...])
    h = x_ref[...] + g1_ref[...] * mixed
    h_ref[...] = h
    ms = jnp.mean(h * h, axis=-1, keepdims=True)
    y = h * lax.rsqrt(ms + EPS) * n2_ref[...]
    hn_ref[...] = (y * (1.0 + sc_ref[...]) + sh_ref[...]).astype(BF16)


def _out_projection(x2d, oa2d, ob2d, mod4, mod_row, norm2, w_out):
    m = x2d.shape[0]
    tm = OUTPROJ_TILE
    half = w_out.shape[0] // 2

    def mod_spec(k):
        return pl.BlockSpec((None, None, 1, D_MODEL), lambda i: (mod_row(i, tm), k, 0, 0))

    def weight_spec(part):
        return pl.BlockSpec((half, D_MODEL), lambda i: (part, 0), pipeline_mode=pl.Buffered(1))

    return pl.pallas_call(
        _outproj_kernel,
        grid=(m // tm,),
        in_specs=[pl.BlockSpec((tm, D_MODEL), lambda i: (i, 0)),
                  pl.BlockSpec((tm, half), lambda i: (i, 0)),
                  pl.BlockSpec((tm, half), lambda i: (i, 0)),
                  mod_spec(2), mod_spec(3), mod_spec(4),
                  pl.BlockSpec((1, D_MODEL), lambda i: (0, 0)),
                  weight_spec(0), weight_spec(1)],
        out_specs=[pl.BlockSpec((tm, D_MODEL), lambda i: (i, 0)),
                   pl.BlockSpec((tm, D_MODEL), lambda i: (i, 0))],
        out_shape=[jax.ShapeDtypeStruct((m, D_MODEL), F32),
                   jax.ShapeDtypeStruct((m, D_MODEL), BF16)],
        compiler_params=_cparams(("parallel",)),
        name="out_projection",
    )(x2d, oa2d, ob2d, mod4, mod4, mod4, norm2, w_out, w_out)


def _mlp_kernel(hn_ref, h_ref, g2_ref, nf_ref, w1_ref, w2_ref, y_ref):
    f = pl.program_id(1)

    @pl.when(f == 0)
    def _():
        y_ref[...] = jnp.zeros_like(y_ref)

    a = jnp.maximum(_dot(hn_ref[...], w1_ref[...]), 0.0)
    a2 = (a * a).astype(BF16)
    for n0 in range(0, D_MODEL, MLP_OUT_CHUNK):
        sl = slice(n0, n0 + MLP_OUT_CHUNK)
        y_ref[:, sl] += _dot(a2, w2_ref[:, sl])

    @pl.when(f == pl.num_programs(1) - 1)
    def _():
        y = h_ref[...] + g2_ref[...] * y_ref[...]
        ms = jnp.mean(y * y, axis=-1, keepdims=True)
        y_ref[...] = y * lax.rsqrt(ms + EPS) * nf_ref[...]


def _mlp(hn2d, h2d, mod4, mod_row, norm_f, w1, w2, tf=512):
    m = h2d.shape[0]
    tm = TOKEN_TILE
    row_tile = pl.BlockSpec((tm, D_MODEL), lambda i, f: (i, 0), pipeline_mode=pl.Buffered(1))
    return pl.pallas_call(
        _mlp_kernel,
        grid=(m // tm, D_FF // tf),
        in_specs=[row_tile, row_tile,
                  pl.BlockSpec((None, None, 1, D_MODEL), lambda i, f: (mod_row(i, tm), 5, 0, 0)),
                  pl.BlockSpec((1, D_MODEL), lambda i, f: (0, 0)),
                  pl.BlockSpec((D_MODEL, tf), lambda i, f: (0, f)),
                  pl.BlockSpec((tf, D_MODEL), lambda i, f: (f, 0))],
        out_specs=pl.BlockSpec((tm, D_MODEL), lambda i, f: (i, 0)),
        out_shape=jax.ShapeDtypeStruct((m, D_MODEL), F32),
        compiler_params=_cparams(("parallel", "arbitrary")),
        name="mlp",
    )(hn2d, h2d, mod4, norm_f, w1, w2)


def _trunk(x, mod4, mod_row, params, ctx):
    (w_main, w_tail, conv_w, alog_row, dtb_row, delta_norm, attn_lambda, attn_subln, norm1,
     w_out, norm2, w_ff1, w_ff2, norm_f) = params
    bsz, seq, _ = x.shape
    x2d = x.reshape(bsz * seq, D_MODEL)
    if ctx is None:
        q, new_k, new_v, main, tail = _in_projection(x2d, mod4, mod_row, norm1, w_main, w_tail, None, seq)
        oa = _diff_attention(q.reshape(bsz, seq, -1), new_k, new_v, None, None, attn_lambda, attn_subln,
                             tq=seq)
    else:
        ctx_kt, ctx_v, s0 = ctx
        new_k = new_v = None
        q, k_t, v, main, tail = _in_projection(x2d, mod4, mod_row, norm1, w_main, w_tail,
                                               _rope_tables(seq), seq)
        oa = _diff_attention(q.reshape(bsz, seq, -1), k_t, v.reshape(bsz, seq, -1), ctx_kt, ctx_v,
                             attn_lambda, attn_subln, tq=ATTN_Q_TILE)
    main3 = main.reshape(bsz, seq, N_MAIN - ATTN_COLS)
    tok3 = _delta_gates(tail.reshape(bsz, seq, LANES), alog_row, dtb_row)
    ob, state = _delta_mixer(main3, tok3, conv_w, delta_norm, None if ctx is None else s0, ctx is None)
    h, hn = _out_projection(x2d, oa.reshape(bsz * seq, -1), ob.reshape(bsz * seq, -1), mod4, mod_row,
                            norm2, w_out)
    y = _mlp(hn, h, mod4, mod_row, norm_f, w_ff1, w_ff2)
    return y.reshape(bsz, seq, D_MODEL), new_k, new_v, state


def kernel(x_prompt, x_sample, cache_attn_k, cache_attn_v, state_delta, c, c_ctx, w_mod, b_mod, norm1, w_in, conv_w, a_log, dt_bias, delta_norm, attn_lambda, attn_subln, w_out, norm2, w_ff1, w_ff2, norm_f):
    bp, lp, _ = x_prompt.shape
    bs, ls, _ = x_sample.shape
    past = cache_attn_k.shape[3]

    rows = 16
    cc = jnp.concatenate([c_ctx[None, :], c, jnp.zeros((rows - 1 - bs, D_MODEL), F32)], axis=0)
    mod = _modulation(cc, w_mod[0], b_mod[0][None, :])
    mod4 = mod.reshape(rows, N_MOD, 1, D_MODEL)

    w_main = w_in[0].astype(BF16)
    w_tail = jnp.pad(w_in[0][:, N_MAIN:], ((0, 0), (0, LANES - N_TAIL))).astype(BF16)
    alog_row = jnp.pad(a_log[0].reshape(1, 2 * H_B), ((0, 0), (0, LANES - 2 * H_B)))
    dtb_row = jnp.pad(dt_bias[0].reshape(1, 2 * H_B), ((0, 0), (0, LANES - 2 * H_B)))
    params = (w_main, w_tail, conv_w[0], alog_row, dtb_row, delta_norm[0][None, :], attn_lambda[0],
              attn_subln[0][None, :], norm1[0][None, :], w_out[0].astype(BF16), norm2[0][None, :],
              w_ff1[0].astype(BF16), w_ff2[0].astype(BF16), norm_f[None, :])

    y_prompt, new_k, new_v, state = _trunk(x_prompt, mod4, lambda i, tm: 0, params, None)
    ctx_kt = cache_attn_k[:, 0].reshape(bs, H_A, past, 2 * HD_A).astype(BF16).transpose(0, 1, 3, 2)
    ctx = (ctx_kt, cache_attn_v[:, 0].astype(BF16), state_delta[:, 0])
    y_sample, _, _, _ = _trunk(x_sample, mod4, lambda i, tm: 1 + (i * tm) // ls, params, ctx)

    new_k = new_k.reshape(bp, 1, H_A, lp, 2, HD_A)
    new_v = new_v.reshape(bp, 1, H_A, lp, DV_A)
    return (y_prompt, y_sample, new_k, new_v, state[:, None])
```
